```python
import math
import jax, jax.numpy as jnp
from jax import lax
import numpy as np

D_MODEL = 1024
BATCH = 2
SEQ = 8192
DEPTH = 2

CHUNK = 64
Q_BLOCK = 128
NORM_EPS = 1e-6

MLA_HEADS = 8
MLA_Q_RANK = 384
MLA_KV_RANK = 256
MLA_NOPE = 64
MLA_ROPE = 32
MLA_QK_DIM = MLA_NOPE + MLA_ROPE
MLA_V = 64
ROPE_THETA = 10000.0

FOX_HEADS = 8
FOX_HEAD_DIM = 64
FOX_WIDTH = FOX_HEADS * FOX_HEAD_DIM

SSM_HEADS = 16
SSM_HEAD_DIM = 64
SSM_INNER = SSM_HEADS * SSM_HEAD_DIM
SSM_GROUPS = 2
SSM_STATE = 128
SSM_CONV = 4
SSM_XBC = SSM_INNER + 2 * SSM_GROUPS * SSM_STATE

D_FF = 2816
FFN_CONV = 3

N_BRANCH = 3
MLA_OUT = MLA_HEADS * MLA_V
MLA_IN = MLA_Q_RANK + MLA_KV_RANK + MLA_ROPE
FOX_IN = 3 * FOX_WIDTH + FOX_HEADS
SSM_IN = SSM_INNER + SSM_XBC + SSM_HEADS
GATE_IN = N_BRANCH * D_MODEL
D_IN = MLA_IN + FOX_IN + SSM_IN + GATE_IN

kernel_name = "hybrid_mla_fox_ssd_gated_block"


def rms_norm(x, gain):
    xf = x.astype(jnp.float32)
    y = xf * lax.rsqrt(jnp.mean(xf * xf, axis=-1, keepdims=True) + NORM_EPS)
    return (y * gain.astype(jnp.float32)).astype(x.dtype)


def causal_dwconv(x, w, b):
    K, C = w.shape
    y = lax.conv_general_dilated(x, w[:, None, :].astype(x.dtype), window_strides=(1,),
                                 padding=[(K - 1, 0)], dimension_numbers=('NWC', 'WIO', 'NWC'),
                                 feature_group_count=C)
    return y + b


def rope_tables(positions):
    inv = 1.0 / (ROPE_THETA ** (jnp.arange(0, MLA_ROPE, 2, dtype=jnp.float32) / MLA_ROPE))
    ang = positions.astype(jnp.float32)[..., None] * inv
    return jnp.cos(ang)[:, :, None, :], jnp.sin(ang)[:, :, None, :]


def rope_tail(x, cos, sin):
    x_nope, x_rope = jnp.split(x, [MLA_NOPE], axis=-1)
    x1, x2 = jnp.split(x_rope.astype(jnp.float32), 2, axis=-1)
    rot = jnp.concatenate([x1 * cos - x2 * sin, x1 * sin + x2 * cos], axis=-1).astype(x.dtype)
    return jnp.concatenate([x_nope, rot], axis=-1)


def blocked_attention(q, k, v, bias_mask):
    Bsz, S, H, Dk = q.shape
    nb = S // Q_BLOCK
    qb = jnp.moveaxis(q.reshape(Bsz, nb, Q_BLOCK, H, Dk), 1, 0)
    scale = Dk ** -0.5

    def one_block(args):
        i, q_blk = args
        s = jnp.einsum('bqhd,bkhd->bhqk', q_blk, k).astype(jnp.float32) * scale
        p = jax.nn.softmax(bias_mask(i, s), axis=-1).astype(v.dtype)
        return jnp.einsum('bhqk,bkhd->bqhd', p, v)

    out = lax.map(one_block, (jnp.arange(nb), qb))
    return jnp.moveaxis(out, 0, 1).reshape(Bsz, S, H, v.shape[-1])


def mla_branch(u, cos, sin, q_norm_g, w_uq, kv_norm_g, w_ukv, q_gain, k_gain):
    Bsz, S, _ = u.shape
    c_q, c_kv, k_rope = jnp.split(u, [MLA_Q_RANK, MLA_Q_RANK + MLA_KV_RANK], axis=-1)
    q = (rms_norm(c_q, q_norm_g) @ w_uq).reshape(Bsz, S, MLA_HEADS, MLA_QK_DIM)
    kv = (rms_norm(c_kv, kv_norm_g) @ w_ukv).reshape(Bsz, S, MLA_HEADS, MLA_NOPE + MLA_V)
    k_nope, v = jnp.split(kv, [MLA_NOPE], axis=-1)
    k_rope = jnp.broadcast_to(k_rope[:, :, None, :], (Bsz, S, MLA_HEADS, MLA_ROPE))
    k = jnp.concatenate([k_nope, k_rope], axis=-1)
    q = rope_tail(rms_norm(q, q_gain), cos, sin)
    k = rope_tail(rms_norm(k, k_gain), cos, sin)

    def chunk_mask(i, s):
        q_idx = i * Q_BLOCK + jnp.arange(Q_BLOCK)
        k_idx = jnp.arange(S)
        allowed = (k_idx[None, :] // CHUNK) <= (q_idx[:, None] // CHUNK)
        return jnp.where(allowed, s, -jnp.inf)

    return blocked_attention(q, k, v, chunk_mask).reshape(Bsz, S, MLA_OUT)


def fox_branch(u, q_gain, k_gain, b_f):
    Bsz, S, _ = u.shape
    q, k, v, f_raw = jnp.split(u, [FOX_WIDTH, 2 * FOX_WIDTH, 3 * FOX_WIDTH], axis=-1)
    q = rms_norm(q.reshape(Bsz, S, FOX_HEADS, FOX_HEAD_DIM), q_gain)
    k = rms_norm(k.reshape(Bsz, S, FOX_HEADS, FOX_HEAD_DIM), k_gain)
    v = v.reshape(Bsz, S, FOX_HEADS, FOX_HEAD_DIM)
    log_f = jax.nn.log_sigmoid(f_raw.astype(jnp.float32) + b_f.astype(jnp.float32))
    F = jnp.transpose(jnp.cumsum(log_f, axis=1), (0, 2, 1))

    def decay_mask(i, s):
        q_idx = i * Q_BLOCK + jnp.arange(Q_BLOCK)
        k_idx = jnp.arange(S)
        Fq = lax.dynamic_slice_in_dim(F, i * Q_BLOCK, Q_BLOCK, axis=2)
        s = s + Fq[..., :, None] - F[..., None, :]
        return jnp.where(k_idx[None, :] <= q_idx[:, None], s, -jnp.inf)

    return blocked_attention(q, k, v, decay_mask).reshape(Bsz, S, FOX_WIDTH)


def ssd_scan(x, dt, A, Bm, Cm):
    Bsz, L, H, P = x.shape
    G, N = Bm.shape[-2:]
    Hg = H // G
    nc = L // CHUNK
    xf = x.astype(jnp.float32).reshape(Bsz, nc, CHUNK, G, Hg, P)
    dtc = dt.reshape(Bsz, nc, CHUNK, G, Hg)
    Bc = Bm.astype(jnp.float32).reshape(Bsz, nc, CHUNK, G, N)
    Cc = Cm.astype(jnp.float32).reshape(Bsz, nc, CHUNK, G, N)
    a_cum = jnp.cumsum(dtc * A.astype(jnp.float32).reshape(G, Hg), axis=2)
    xdt = xf * dtc[..., None]
    causal = jnp.tril(jnp.ones((CHUNK, CHUNK), dtype=bool))[None, None, :, :, None, None]
    seg = a_cum[:, :, :, None] - a_cum[:, :, None, :]
    decay = jnp.where(causal, jnp.exp(jnp.where(causal, seg, 0.0)), 0.0)
    cb = jnp.einsum('bcign,bcjgn->bcijg', Cc, Bc)
    y_diag = jnp.einsum('bcijg,bcijgh,bcjghp->bcighp', cb, decay, xdt)
    decay_to_end = jnp.exp(a_cum[:, :, -1:] - a_cum)
    states = jnp.einsum('bcjgn,bcjgh,bcjghp->bcghpn', Bc, decay_to_end, xdt)
    chunk_decay = jnp.exp(a_cum[:, :, -1])

    def step(h, inp):
        st, dec = inp
        return h * dec[..., None, None] + st, h

    h0 = jnp.zeros((Bsz, G, Hg, P, N), jnp.float32)
    _, h_in = lax.scan(step, h0, (jnp.moveaxis(states, 1, 0), jnp.moveaxis(chunk_decay, 1, 0)))
    h_in = jnp.moveaxis(h_in, 0, 1)
    y_off = jnp.einsum('bcign,bcghpn,bcigh->bcighp', Cc, h_in, jnp.exp(a_cum))
    return (y_diag + y_off).reshape(Bsz, L, H, P).astype(x.dtype)


def ssm_branch(u, conv_w, conv_b, dt_bias, A_log, D, norm_g):
    Bsz, S, _ = u.shape
    z, xBC, dt_raw = jnp.split(u, [SSM_INNER, SSM_INNER + SSM_XBC], axis=-1)
    xBC = jax.nn.silu(causal_dwconv(xBC, conv_w, conv_b))
    xs, Bm, Cm = jnp.split(xBC, [SSM_INNER, SSM_INNER + SSM_GROUPS * SSM_STATE], axis=-1)
    dt = jax.nn.softplus(dt_raw.astype(jnp.float32) + dt_bias.astype(jnp.float32))
    A = -jnp.exp(A_log.astype(jnp.float32))
    xh = xs.reshape(Bsz, S, SSM_HEADS, SSM_HEAD_DIM)
    y = ssd_scan(xh, dt, A, Bm.reshape(Bsz, S, SSM_GROUPS, SSM_STATE),
                 Cm.reshape(Bsz, S, SSM_GROUPS, SSM_STATE))
    y = (y + xh * D[:, None]).reshape(Bsz, S, SSM_INNER)
    return rms_norm(y * jax.nn.silu(z), norm_g)


def conv_ffn(h, w_up, conv_w, conv_b, w_down):
    a = causal_dwconv(h @ w_up, conv_w, conv_b)
    gate, val = jnp.split(a, 2, axis=-1)
    return (jax.nn.silu(gate) * val) @ w_down


def setup_inputs(seed: int = 0) -> dict:
    key = jax.random.key(seed)
    ks = iter(jax.random.split(key, 48))
    f32 = jnp.float32
    L = DEPTH

    def nrm(shape, scale):
        return scale * jax.random.normal(next(ks), shape, f32)

    def gain(shape):
        return 1.0 + 0.02 * jax.random.normal(next(ks), shape, f32)

    x = jax.random.normal(next(ks), (BATCH, SEQ, D_MODEL), f32)
    steps = jax.random.randint(next(ks), (BATCH, SEQ), 1, 3, dtype=jnp.int32)
    positions = (jnp.cumsum(steps, axis=1) - 1).astype(jnp.int32)
    dt0 = jnp.exp(jax.random.uniform(next(ks), (L, SSM_HEADS), f32, math.log(1e-3), math.log(1e-1)))
    return {
        'x': x,
        'positions': positions,
        'norm_mix_g': gain((L, D_MODEL)),
        'w_in': nrm((L, D_MODEL, D_IN), D_MODEL ** -0.5),
        'b_gate': nrm((L, GATE_IN), 0.01),
        'mla_q_norm_g': gain((L, MLA_Q_RANK)),
        'mla_w_uq': nrm((L, MLA_Q_RANK, MLA_HEADS * MLA_QK_DIM), MLA_Q_RANK ** -0.5),
        'mla_kv_norm_g': gain((L, MLA_KV_RANK)),
        'mla_w_ukv': nrm((L, MLA_KV_RANK, MLA_HEADS * (MLA_NOPE + MLA_V)), MLA_KV_RANK ** -0.5),
        'mla_q_gain': gain((L, MLA_QK_DIM)),
        'mla_k_gain': gain((L, MLA_QK_DIM)),
        'fox_q_gain': gain((L, FOX_HEAD_DIM)),
        'fox_k_gain': gain((L, FOX_HEAD_DIM)),
        'fox_b_f': jax.random.uniform(next(ks), (L, FOX_HEADS), f32, 2.0, 6.0),
        'ssm_conv_w': nrm((L, SSM_CONV, SSM_XBC), SSM_CONV ** -0.5),
        'ssm_conv_b': nrm((L, SSM_XBC), 0.01),
        'ssm_dt_bias': dt0 + jnp.log(-jnp.expm1(-dt0)),
        'ssm_A_log': jnp.log(jax.random.uniform(next(ks), (L, SSM_HEADS), f32, 1.0, 16.0)),
        'ssm_D': gain((L, SSM_HEADS)),
        'ssm_norm_g': gain((L, SSM_INNER)),
        'w_br_mla': nrm((L, MLA_OUT, D_MODEL), MLA_OUT ** -0.5),
        'w_br_fox': nrm((L, FOX_WIDTH, D_MODEL), FOX_WIDTH ** -0.5),
        'w_br_ssm': nrm((L, SSM_INNER, D_MODEL), SSM_INNER ** -0.5),
        'w_out': nrm((L, D_MODEL, D_MODEL), D_MODEL ** -0.5),
        'norm_ffn_g': gain((L, D_MODEL)),
        'ffn_w_up': nrm((L, D_MODEL, 2 * D_FF), D_MODEL ** -0.5),
        'ffn_conv_w': nrm((L, FFN_CONV, 2 * D_FF), FFN_CONV ** -0.5),
        'ffn_conv_b': nrm((L, 2 * D_FF), 0.01),
        'ffn_w_down': nrm((L, D_FF, D_MODEL), D_FF ** -0.5),
    }


def reference(x, positions, norm_mix_g, w_in, b_gate, mla_q_norm_g, mla_w_uq, mla_kv_norm_g,
              mla_w_ukv, mla_q_gain, mla_k_gain, fox_q_gain, fox_k_gain, fox_b_f, ssm_conv_w,
              ssm_conv_b, ssm_dt_bias, ssm_A_log, ssm_D, ssm_norm_g, w_br_mla, w_br_fox, w_br_ssm,
              w_out, norm_ffn_g, ffn_w_up, ffn_conv_w, ffn_conv_b, ffn_w_down):
    Bsz, S, _ = x.shape
    cos, sin = rope_tables(positions)
    split_at = [MLA_IN, MLA_IN + FOX_IN, MLA_IN + FOX_IN + SSM_IN]
    for l in range(DEPTH):
        hn = rms_norm(x, norm_mix_g[l])
        u = hn @ w_in[l]
        u_mla, u_fox, u_ssm, u_gate = jnp.split(u, split_at, axis=-1)
        y_a = mla_branch(u_mla, cos, sin, mla_q_norm_g[l], mla_w_uq[l], mla_kv_norm_g[l],
                         mla_w_ukv[l], mla_q_gain[l], mla_k_gain[l]) @ w_br_mla[l]
        y_b = fox_branch(u_fox, fox_q_gain[l], fox_k_gain[l], fox_b_f[l]) @ w_br_fox[l]
        y_c = ssm_branch(u_ssm, ssm_conv_w[l], ssm_conv_b[l], ssm_dt_bias[l], ssm_A_log[l],
                         ssm_D[l], ssm_norm_g[l]) @ w_br_ssm[l]
        g = jax.nn.sigmoid(u_gate + b_gate[l]).reshape(Bsz, S, N_BRANCH, D_MODEL)
        merged = g[:, :, 0] * y_a + g[:, :, 1] * y_b + g[:, :, 2] * y_c
        x = x + merged @ w_out[l]
        x = x + conv_ffn(rms_norm(x, norm_ffn_g[l]), ffn_w_up[l], ffn_conv_w[l], ffn_conv_b[l],
                         ffn_w_down[l])
    return x
```

```python
import functools
import math

import jax
import jax.numpy as jnp
from jax import lax
from jax.experimental import pallas as pl
from jax.experimental.pallas import tpu as pltpu

F32 = jnp.float32
BF16 = jnp.bfloat16

D_MODEL = 1024
NORM_EPS = 1e-6
CHUNK = 64

MLA_HEADS = 8
MLA_Q_RANK = 384
MLA_KV_RANK = 256
MLA_NOPE = 64
MLA_ROPE = 32
MLA_QK_DIM = MLA_NOPE + MLA_ROPE
MLA_V = 64
ROPE_THETA = 10000.0

FOX_HEADS = 8
FOX_HEAD_DIM = 64
FOX_WIDTH = FOX_HEADS * FOX_HEAD_DIM

SSM_HEADS = 16
SSM_HEAD_DIM = 64
SSM_INNER = SSM_HEADS * SSM_HEAD_DIM
SSM_GROUPS = 2
SSM_STATE = 128
SSM_CONV = 4
SSM_XBC = SSM_INNER + 2 * SSM_GROUPS * SSM_STATE
SSM_GROUP_WIDTH = SSM_INNER // SSM_GROUPS

D_FF = 2816
FFN_CONV = 3

MLA_IN = MLA_Q_RANK + MLA_KV_RANK + MLA_ROPE
FOX_IN = 3 * FOX_WIDTH + FOX_HEADS
SSM_IN = SSM_INNER + SSM_XBC + SSM_HEADS

LANES = 128
SUBLANES = 8
HEAD_PAD = LANES
LOG2E = 1.4426950408889634
NEG_BIG = -1e30
VMEM_LIMIT = 52 * 1024 * 1024

MLA_SEG = MLA_Q_RANK + MLA_KV_RANK + LANES
SEG_MLA = (0, MLA_SEG)
SEG_FOX = (SEG_MLA[1], SEG_MLA[1] + 3 * FOX_WIDTH)
SEG_Z = (SEG_FOX[1], SEG_FOX[1] + SSM_INNER)
SEG_XBC = (SEG_Z[1], SEG_Z[1] + SSM_XBC)
SEG_GATE = (SEG_XBC[1], SEG_XBC[1] + 3 * D_MODEL)
W_CAT = SEG_GATE[1]
SMALL_W = 2 * LANES

AUG0 = FOX_HEAD_DIM


def _cparams(n_grid):
    return pltpu.CompilerParams(dimension_semantics=("arbitrary",) * n_grid,
                                vmem_limit_bytes=VMEM_LIMIT)


def _vmem_full():
    return pl.BlockSpec(memory_space=pltpu.VMEM)


def _dot(a, b):
    return jnp.dot(a, b, preferred_element_type=F32)


def _dot_nt(a, b):
    return lax.dot_general(a, b, (((1,), (1,)), ((), ())), preferred_element_type=F32)


def _split3(x):
    hi = x.astype(BF16)
    r = x - hi.astype(F32)
    mid = r.astype(BF16)
    lo = (r - mid.astype(F32)).astype(BF16)
    return hi, mid, lo


def _dot3_left(m01, x):
    hi, mid, lo = _split3(x)
    return _dot(m01, hi) + _dot(m01, mid) + _dot(m01, lo)


def _dot3_right(x, m01):
    hi, mid, lo = _split3(x)
    return _dot(hi, m01) + _dot(mid, m01) + _dot(lo, m01)


def _sigmoid(x):
    return 1.0 / (1.0 + jnp.exp(-x))


def _silu(x):
    return x * _sigmoid(x)


def _softplus(x):
    return jnp.maximum(x, 0.0) + jnp.log(1.0 + jnp.exp(-jnp.abs(x)))


def _tril_bf16(n):
    r = lax.broadcasted_iota(jnp.int32, (n, n), 0)
    c = lax.broadcasted_iota(jnp.int32, (n, n), 1)
    return jnp.where(r >= c, 1.0, 0.0).astype(BF16)


def _rope_kernel(pos_ref, inv_ref, cos_ref, sa_ref, sb_ref):
    ang = pos_ref[...] * inv_ref[...]
    lane = lax.broadcasted_iota(jnp.int32, ang.shape, 1)
    c = jnp.cos(ang)
    s = jnp.sin(ang)
    half = MLA_ROPE // 2
    first = (lane >= MLA_NOPE) & (lane < MLA_NOPE + half)
    second = (lane >= MLA_NOPE + half) & (lane < MLA_QK_DIM)
    cos_ref[...] = jnp.where(lane < MLA_NOPE, 1.0, jnp.where(first | second, c, 0.0))
    sa_ref[...] = jnp.where(first, -s, 0.0)
    sb_ref[...] = jnp.where(second, s, 0.0)


def _rope_tables(pos, ts):
    T = pos.shape[0]
    half = MLA_ROPE // 2
    inv = 1.0 / (ROPE_THETA ** (jnp.arange(0, MLA_ROPE, 2, dtype=F32) / MLA_ROPE))
    inv_lane = jnp.zeros((1, LANES), F32)
    inv_lane = inv_lane.at[0, MLA_NOPE:MLA_NOPE + half].set(inv)
    inv_lane = inv_lane.at[0, MLA_NOPE + half:MLA_QK_DIM].set(inv)
    tab = jax.ShapeDtypeStruct((T, LANES), F32)
    spec = pl.BlockSpec((ts, LANES), lambda i: (i, 0))
    return pl.pallas_call(
        _rope_kernel,
        grid=(T // ts,),
        in_specs=[pl.BlockSpec((ts, 1), lambda i: (i, 0)), pl.BlockSpec((1, LANES), lambda i: (0, 0))],
        out_specs=[spec, spec, spec],
        out_shape=[tab, tab, tab],
        compiler_params=_cparams(1),
        name="rope_tables",
    )(pos, inv_lane)


def _rope(x, cos, sa, sb):
    half = MLA_ROPE // 2
    return x * cos + pltpu.roll(x, LANES - half, axis=1) * sa + pltpu.roll(x, half, axis=1) * sb


IN_CH = 512


def _inproj_kernel(x_ref, g_ref, w_ref, wsm_ref, o_mla, o_fox, o_z, o_xbc, o_gate, o_small):
    x = x_ref[...]
    hn = (x * lax.rsqrt(jnp.mean(x * x, axis=-1, keepdims=True) + NORM_EPS) * g_ref[...]).astype(BF16)
    for o_ref, (c0, c1) in ((o_mla, SEG_MLA), (o_fox, SEG_FOX), (o_z, SEG_Z), (o_xbc, SEG_XBC),
                            (o_gate, SEG_GATE)):
        for cc in range(c0, c1, IN_CH):
            w = min(IN_CH, c1 - cc)
            o_ref[:, cc - c0:cc - c0 + w] = _dot(hn, w_ref[:, cc:cc + w]).astype(o_ref.dtype)
    o_small[...] = _dot(hn, wsm_ref[...])


def _inproj(x2, g, w_cat, w_small, tm):
    T = x2.shape[0]
    widths = [s[1] - s[0] for s in (SEG_MLA, SEG_FOX, SEG_Z, SEG_XBC, SEG_GATE)]
    out_shape = [jax.ShapeDtypeStruct((T, w), BF16) for w in widths]
    out_shape.append(jax.ShapeDtypeStruct((T, SMALL_W), F32))
    out_specs = [pl.BlockSpec((tm, w), lambda i: (i, 0)) for w in widths]
    out_specs.append(pl.BlockSpec((tm, SMALL_W), lambda i: (i, 0)))
    return pl.pallas_call(
        _inproj_kernel,
        grid=(T // tm,),
        in_specs=[pl.BlockSpec((tm, D_MODEL), lambda i: (i, 0)),
                  pl.BlockSpec((1, D_MODEL), lambda i: (0, 0)),
                  _vmem_full(), _vmem_full()],
        out_specs=out_specs,
        out_shape=out_shape,
        compiler_params=_cparams(1),
        name="inproj",
    )(x2, g, w_cat, w_small)


def _mla_prep_kernel(u_ref, cos_ref, sa_ref, sb_ref, gq_ref, wq_ref, gkv_ref, wk_ref, wv_ref,
                     qg_ref, kg_ref, q_out, k_out, vt_out):
    u = u_ref[...].astype(F32)
    cos = cos_ref[...]
    sa = sa_ref[...]
    sb = sb_ref[...]
    cq = u[:, :MLA_Q_RANK]
    ckv = u[:, MLA_Q_RANK:MLA_Q_RANK + MLA_KV_RANK]
    kr = u[:, MLA_Q_RANK + MLA_KV_RANK:]
    cqn = (cq * lax.rsqrt(jnp.mean(cq * cq, axis=-1, keepdims=True) + NORM_EPS) * gq_ref[...]).astype(BF16)
    ckvn = (ckv * lax.rsqrt(jnp.mean(ckv * ckv, axis=-1, keepdims=True) + NORM_EPS) * gkv_ref[...]).astype(BF16)
    q_all = _dot(cqn, wq_ref[...])
    k_all = _dot(ckvn, wk_ref[...])
    v_all = _dot(ckvn, wv_ref[...])
    kr_ss = jnp.sum(kr * kr, axis=-1, keepdims=True)
    q_scale = (MLA_QK_DIM ** -0.5) * LOG2E
    for h in range(MLA_HEADS):
        qh = q_all[:, h * HEAD_PAD:(h + 1) * HEAD_PAD]
        ms = jnp.sum(qh * qh, axis=-1, keepdims=True) * (1.0 / MLA_QK_DIM)
        qh = qh * lax.rsqrt(ms + NORM_EPS) * qg_ref[...]
        q_out[0, h] = (_rope(qh, cos, sa, sb) * q_scale).astype(q_out.dtype)
        kn = k_all[:, h * HEAD_PAD:(h + 1) * HEAD_PAD]
        ms = (jnp.sum(kn * kn, axis=-1, keepdims=True) + kr_ss) * (1.0 / MLA_QK_DIM)
        kh = (kn + kr) * lax.rsqrt(ms + NORM_EPS) * kg_ref[...]
        k_out[0, h] = _rope(kh, cos, sa, sb).astype(k_out.dtype)
    vt = v_all.T
    vt_out[0, :, 0] = vt.reshape(MLA_HEADS, MLA_V, vt.shape[-1]).astype(vt_out.dtype)


def _mla_prep(u_mla, cos, sa, sb, gq, wq, gkv, wk, wv, qg, kg, B, S, ts):
    nb = S // ts
    H = MLA_HEADS
    tok = lambda b, i: (b * nb + i, 0)
    const = lambda b, i: (0, 0)
    return pl.pallas_call(
        _mla_prep_kernel,
        grid=(B, nb),
        in_specs=[pl.BlockSpec((ts, MLA_SEG), tok),
                  pl.BlockSpec((ts, LANES), tok), pl.BlockSpec((ts, LANES), tok), pl.BlockSpec((ts, LANES), tok),
                  pl.BlockSpec((1, MLA_Q_RANK), const), _vmem_full(),
                  pl.BlockSpec((1, MLA_KV_RANK), const), _vmem_full(), _vmem_full(),
                  pl.BlockSpec((1, LANES), const), pl.BlockSpec((1, LANES), const)],
        out_specs=[pl.BlockSpec((1, H, ts, HEAD_PAD), lambda b, i: (b, 0, i, 0)),
                   pl.BlockSpec((1, H, ts, HEAD_PAD), lambda b, i: (b, 0, i, 0)),
                   pl.BlockSpec((1, H, 1, MLA_V, ts), lambda b, i: (b, 0, i, 0, 0))],
        out_shape=[jax.ShapeDtypeStruct((B, H, S, HEAD_PAD), BF16),
                   jax.ShapeDtypeStruct((B, H, S, HEAD_PAD), BF16),
                   jax.ShapeDtypeStruct((B, H, nb, MLA_V, ts), BF16)],
        compiler_params=_cparams(2),
        name="mla_prep",
    )(u_mla, cos, sa, sb, gq, wq, gkv, wk, wv, qg, kg)


def _fox_prep_kernel(u_ref, f_ref, bf_ref, qg_ref, kg_ref, pq_ref, pk_ref, oneq_ref, onek_ref,
                     q_out, k_out, vt_out, carry):
    ts = u_ref.shape[0]

    @pl.when(pl.program_id(1) == 0)
    def _():
        carry[...] = jnp.zeros_like(carry)

    lane = lax.broadcasted_iota(jnp.int32, (ts, LANES), 1)
    f = f_ref[...] + bf_ref[...]
    log_f = jnp.minimum(f, 0.0) - jnp.log(1.0 + jnp.exp(-jnp.abs(f)))
    log_f = jnp.where(lane < FOX_HEADS, log_f * LOG2E, 0.0)
    cum = carry[...] + _dot3_left(_tril_bf16(ts), log_f)
    carry[...] = cum[ts - 1:ts, :]
    hi, mid, lo = _split3(cum)
    packed = (hi.astype(F32) + pltpu.roll(mid.astype(F32), FOX_HEADS, axis=1)
              + pltpu.roll(lo.astype(F32), 2 * FOX_HEADS, axis=1)).astype(BF16)
    aug_q = _dot(packed, pq_ref[...])
    aug_k = _dot(packed, pk_ref[...])

    u = u_ref[...].astype(F32)
    q_scale = (FOX_HEAD_DIM ** -0.5) * LOG2E
    head_lanes = lane < FOX_HEAD_DIM
    for h in range(FOX_HEADS):
        t0 = (h // 2) * LANES
        for src, gain, aug, ones, scale, out in (
                (0, qg_ref, aug_q, oneq_ref, q_scale, q_out),
                (FOX_WIDTH, kg_ref, aug_k, onek_ref, 1.0, k_out)):
            blk = u[:, src + t0:src + t0 + LANES]
            if h % 2 == 1:
                blk = pltpu.roll(blk, FOX_HEAD_DIM, axis=1)
            blk = jnp.where(head_lanes, blk, 0.0)
            ms = jnp.sum(blk * blk, axis=-1, keepdims=True) * (1.0 / FOX_HEAD_DIM)
            blk = blk * lax.rsqrt(ms + NORM_EPS) * (gain[...] * scale)
            blk = blk + aug[:, h * HEAD_PAD:(h + 1) * HEAD_PAD] + ones[...]
            out[0, h] = blk.astype(out.dtype)
    vt = u[:, 2 * FOX_WIDTH:].T
    vt_out[0, :, 0] = vt.reshape(FOX_HEADS, FOX_HEAD_DIM, ts).astype(vt_out.dtype)


def _fox_prep(u_fox, u_small, bf, qg, kg, pq, pk, oneq, onek, B, S, ts):
    nb = S // ts
    H = FOX_HEADS
    tok = lambda b, i: (b * nb + i, 0)
    const = lambda b, i: (0, 0)
    return pl.pallas_call(
        _fox_prep_kernel,
        grid=(B, nb),
        in_specs=[pl.BlockSpec((ts, 3 * FOX_WIDTH), tok),
                  pl.BlockSpec((ts, LANES), tok),
                  pl.BlockSpec((1, LANES), const), pl.BlockSpec((1, LANES), const),
                  pl.BlockSpec((1, LANES), const), _vmem_full(), _vmem_full(),
                  pl.BlockSpec((1, LANES), const), pl.BlockSpec((1, LANES), const)],
        out_specs=[pl.BlockSpec((1, H, ts, HEAD_PAD), lambda b, i: (b, 0, i, 0)),
                   pl.BlockSpec((1, H, ts, HEAD_PAD), lambda b, i: (b, 0, i, 0)),
                   pl.BlockSpec((1, H, 1, FOX_HEAD_DIM, ts), lambda b, i: (b, 0, i, 0, 0))],
        out_shape=[jax.ShapeDtypeStruct((B, H, S, HEAD_PAD), BF16),
                   jax.ShapeDtypeStruct((B, H, S, HEAD_PAD), BF16),
                   jax.ShapeDtypeStruct((B, H, nb, FOX_HEAD_DIM, ts), BF16)],
        scratch_shapes=[pltpu.VMEM((1, LANES), F32)],
        compiler_params=_cparams(2),
        name="fox_prep",
    )(u_fox, u_small, bf, qg, kg, pq, pk, oneq, onek)


HEADS_PER_STEP = 2


def _attn_kernel(q_ref, k_ref, vt_ref, o_ref, *, tq, chunk_causal):
    i = pl.program_id(2)
    dv = vt_ref.shape[3]

    def tile(hh, j, carry, diagonal):
        m, l, acc = carry
        q = q_ref[0, hh]
        kj = k_ref[0, hh, pl.ds(pl.multiple_of(j * tq, tq), tq), :]
        s = _dot_nt(kj, q)
        if diagonal:
            r = lax.broadcasted_iota(jnp.int32, s.shape, 0)
            c = lax.broadcasted_iota(jnp.int32, s.shape, 1)
            shift = CHUNK.bit_length() - 1
            ok = ((r >> shift) <= (c >> shift)) if chunk_causal else (r <= c)
            s = jnp.where(ok, s, NEG_BIG)
        m_new = jnp.maximum(m, jnp.max(s, axis=0, keepdims=True))
        alpha = jnp.exp2(m - m_new)
        p = jnp.exp2(s - m_new)
        l = alpha * l + jnp.sum(p, axis=0, keepdims=True)
        acc = alpha * acc + _dot(vt_ref[0, hh, j], p.astype(BF16))
        return m_new, l, acc

    outs = []
    for hh in range(HEADS_PER_STEP):
        init = (jnp.full((1, tq), NEG_BIG, F32), jnp.zeros((1, tq), F32), jnp.zeros((dv, tq), F32))
        carry = lax.fori_loop(0, i, lambda j, c: tile(hh, j, c, False), init)
        m, l, acc = tile(hh, i, carry, True)
        outs.append(acc * (1.0 / l))
    o = jnp.concatenate(outs, axis=0)
    o_ref[0] = o.T.astype(o_ref.dtype)


def _attention(q, k, vt, tq, chunk_causal):
    B, H, S, _ = q.shape
    dv = vt.shape[3]
    assert vt.shape[4] == tq and HEADS_PER_STEP * dv == LANES
    nq = S // tq
    hp = HEADS_PER_STEP
    return pl.pallas_call(
        functools.partial(_attn_kernel, tq=tq, chunk_causal=chunk_causal),
        grid=(B, H // hp, nq),
        in_specs=[pl.BlockSpec((1, hp, tq, HEAD_PAD), lambda b, h, i: (b, h, i, 0)),
                  pl.BlockSpec((1, hp, S, HEAD_PAD), lambda b, h, i: (b, h, 0, 0)),
                  pl.BlockSpec((1, hp, nq, dv, tq), lambda b, h, i: (b, h, 0, 0, 0))],
        out_specs=pl.BlockSpec((1, tq, LANES), lambda b, h, i: (b, i, h)),
        out_shape=jax.ShapeDtypeStruct((B, S, H * dv), BF16),
        compiler_params=_cparams(3),
        name="attn_mla" if chunk_causal else "attn_fox",
    )(q, k, vt)


def _ssm_kernel(z_ref, xbc_ref, dt_ref, cw_ref, cb_ref, dtb_ref, alog_ref, dx_ref, ng_ref, e_ref,
                o_ref, xpad, hst, *, lc):
    pad = SUBLANES

    @pl.when(pl.program_id(1) == 0)
    def _():
        xpad[0:pad, :] = jnp.zeros((pad, SSM_XBC), F32)
        hst[...] = jnp.zeros_like(hst)

    xpad[pad:pad + lc, :] = xbc_ref[...].astype(F32)
    conv = cb_ref[...]
    for kk in range(SSM_CONV):
        off = pad - (SSM_CONV - 1) + kk
        conv = conv + cw_ref[kk:kk + 1, :] * xpad[off:off + lc, :]
    xpad[0:pad, :] = xpad[lc:lc + pad, :]
    xc = _silu(conv)
    xs = xc[:, :SSM_INNER]
    bm = xc[:, SSM_INNER:SSM_INNER + SSM_GROUPS * SSM_STATE]
    cm = xc[:, SSM_INNER + SSM_GROUPS * SSM_STATE:]

    lane = lax.broadcasted_iota(jnp.int32, (lc, LANES), 1)
    dt = jnp.where(lane < SSM_HEADS, _softplus(dt_ref[...] + dtb_ref[...]), 0.0)
    a_neg = -jnp.exp(alog_ref[...])
    a_cum = _dot3_left(_tril_bf16(lc), dt * a_neg)
    ea = jnp.exp(a_cum)
    dte = jnp.exp(a_cum[lc - 1:lc, :] - a_cum)
    e01 = e_ref[...]
    dt_x = _dot3_right(dt, e01)
    ea_x = _dot3_right(ea, e01)
    dte_x = _dot3_right(dte, e01)
    xdt = xs * dt_x
    a_t = a_cum.T

    row = lax.broadcasted_iota(jnp.int32, (lc, lc), 0)
    col = lax.broadcasted_iota(jnp.int32, (lc, lc), 1)
    causal = row >= col
    lane_first = lane < SSM_HEAD_DIM
    heads_per_group = SSM_HEADS // SSM_GROUPS
    y_blocks = []
    for g in range(SSM_GROUPS):
        bg = bm[:, g * SSM_STATE:(g + 1) * SSM_STATE]
        cg = cm[:, g * SSM_STATE:(g + 1) * SSM_STATE].astype(BF16)
        cb = _dot_nt(cg, bg.astype(BF16))
        gs = slice(g * SSM_GROUP_WIDTH, (g + 1) * SSM_GROUP_WIDTH)
        h_prev = hst[g]
        y_off = _dot(cg, h_prev.astype(BF16)) * ea_x[:, gs]
        xw = (xdt[:, gs] * dte_x[:, gs]).astype(BF16)
        s_new = _dot(bg.T.astype(BF16), xw)
        hst[g] = h_prev * ea_x[lc - 1:lc, gs] + s_new
        for hp in range(heads_per_group // 2):
            t0 = g * SSM_GROUP_WIDTH + hp * LANES
            x_pair = xdt[:, t0:t0 + LANES]
            acc = None
            for e in range(2):
                h = g * heads_per_group + hp * 2 + e
                seg = a_cum[:, h:h + 1] - a_t[h:h + 1, :]
                decay = jnp.exp(jnp.where(causal, seg, NEG_BIG))
                gmat = (cb * decay).astype(BF16)
                xh = jnp.where(lane_first if e == 0 else ~lane_first, x_pair, 0.0).astype(BF16)
                part = _dot(gmat, xh)
                acc = part if acc is None else acc + part
            y_blocks.append(acc + y_off[:, hp * LANES:(hp + 1) * LANES])
    y = jnp.concatenate(y_blocks, axis=1) + xs * dx_ref[...]
    zf = z_ref[...].astype(F32)
    yg = y * _silu(zf)
    o_ref[...] = (yg * lax.rsqrt(jnp.mean(yg * yg, axis=-1, keepdims=True) + NORM_EPS)
                  * ng_ref[...]).astype(o_ref.dtype)


def _ssm(u_z, u_xbc, u_small, cw, cb, dtb, alog, dx, ng, e01, B, S, lc):
    nb = S // lc
    tok = lambda b, i: (b * nb + i, 0)
    const = lambda b, i: (0, 0)
    return pl.pallas_call(
        functools.partial(_ssm_kernel, lc=lc),
        grid=(B, nb),
        in_specs=[pl.BlockSpec((lc, SSM_INNER), tok),
                  pl.BlockSpec((lc, SSM_XBC), tok),
                  pl.BlockSpec((lc, LANES), lambda b, i: (b * nb + i, 1)),
                  pl.BlockSpec((SSM_CONV, SSM_XBC), const), pl.BlockSpec((1, SSM_XBC), const),
                  pl.BlockSpec((1, LANES), const), pl.BlockSpec((1, LANES), const),
                  pl.BlockSpec((1, SSM_INNER), const), pl.BlockSpec((1, SSM_INNER), const),
                  _vmem_full()],
        out_specs=pl.BlockSpec((lc, SSM_INNER), tok),
        out_shape=jax.ShapeDtypeStruct((B * S, SSM_INNER), BF16),
        scratch_shapes=[pltpu.VMEM((lc + SUBLANES, SSM_XBC), F32),
                        pltpu.VMEM((SSM_GROUPS, SSM_STATE, SSM_GROUP_WIDTH), F32)],
        compiler_params=_cparams(2),
        name="ssm",
    )(u_z, u_xbc, u_small, cw, cb, dtb, alog, dx, ng, e01)


def _merge_kernel(x_ref, ya_ref, yb_ref, yc_ref, ug_ref, bg_ref, wa_ref, wb_ref, wc_ref, wo_ref, o_ref):
    ys = (_dot(ya_ref[...], wa_ref[...]), _dot(yb_ref[...], wb_ref[...]), _dot(yc_ref[...], wc_ref[...]))
    merged = None
    for n, y in enumerate(ys):
        cs = slice(n * D_MODEL, (n + 1) * D_MODEL)
        gate = _sigmoid(ug_ref[:, cs].astype(F32) + bg_ref[:, cs])
        merged = gate * y if merged is None else merged + gate * y
    o_ref[...] = x_ref[...] + _dot(merged.astype(BF16), wo_ref[...])


def _merge(x2, ya, yb, yc, ug, bg, wa, wb, wc, wo, tm):
    T = x2.shape[0]
    row = lambda w: pl.BlockSpec((tm, w), lambda i: (i, 0))
    return pl.pallas_call(
        _merge_kernel,
        grid=(T // tm,),
        in_specs=[row(D_MODEL), row(MLA_HEADS * MLA_V), row(FOX_WIDTH), row(SSM_INNER), row(3 * D_MODEL),
                  pl.BlockSpec((1, 3 * D_MODEL), lambda i: (0, 0)),
                  _vmem_full(), _vmem_full(), _vmem_full(), _vmem_full()],
        out_specs=row(D_MODEL),
        out_shape=jax.ShapeDtypeStruct((T, D_MODEL), F32),
        compiler_params=_cparams(1),
        name="merge",
    )(x2, ya, yb, yc, ug, bg, wa, wb, wc, wo)


FF_CH = 256


def _ffn_kernel(x_ref, g_ref, wup_ref, cw_ref, cb_ref, wdn_ref, o_ref, abuf, hbuf, *, tm, blocks_per_seq):
    pad = SUBLANES

    @pl.when(pl.program_id(0) % blocks_per_seq == 0)
    def _():
        abuf[0:pad, :] = jnp.zeros((pad, 2 * D_FF), F32)

    x = x_ref[...]
    hn = (x * lax.rsqrt(jnp.mean(x * x, axis=-1, keepdims=True) + NORM_EPS) * g_ref[...]).astype(BF16)
    for c0 in range(0, D_FF, FF_CH):
        halves = []
        for base in (c0, D_FF + c0):
            cs = slice(base, base + FF_CH)
            abuf[pad:pad + tm, cs] = _dot(hn, wup_ref[:, cs])
            conv = cb_ref[:, cs]
            for kk in range(FFN_CONV):
                off = pad - (FFN_CONV - 1) + kk
                conv = conv + cw_ref[kk:kk + 1, cs] * abuf[off:off + tm, cs]
            abuf[0:pad, cs] = abuf[tm:tm + pad, cs]
            halves.append(conv)
        hbuf[:, c0:c0 + FF_CH] = (_silu(halves[0]) * halves[1]).astype(BF16)
    o_ref[...] = x + _dot(hbuf[...], wdn_ref[...])


def _ffn(x2, g, wup, cw, cb, wdn, S, tm):
    T = x2.shape[0]
    return pl.pallas_call(
        functools.partial(_ffn_kernel, tm=tm, blocks_per_seq=S // tm),
        grid=(T // tm,),
        in_specs=[pl.BlockSpec((tm, D_MODEL), lambda i: (i, 0)),
                  pl.BlockSpec((1, D_MODEL), lambda i: (0, 0)),
                  _vmem_full(),
                  pl.BlockSpec((FFN_CONV, 2 * D_FF), lambda i: (0, 0)),
                  pl.BlockSpec((1, 2 * D_FF), lambda i: (0, 0)),
                  _vmem_full()],
        out_specs=pl.BlockSpec((tm, D_MODEL), lambda i: (i, 0)),
        out_shape=jax.ShapeDtypeStruct((T, D_MODEL), F32),
        scratch_shapes=[pltpu.VMEM((tm + SUBLANES, 2 * D_FF), F32),
                        pltpu.VMEM((tm, D_FF), BF16)],
        compiler_params=_cparams(1),
        name="ffn",
    )(x2, g, wup, cw, cb, wdn)


def _pad_lanes(v, width=LANES, offset=0):
    out = jnp.zeros((1, width), F32)
    return out.at[0, offset:offset + v.shape[0]].set(v.astype(F32))


def _pack_w_in(w):
    zeros = lambda n: jnp.zeros((D_MODEL, n), w.dtype)
    o_fox = MLA_IN
    o_ssm = MLA_IN + FOX_IN
    o_gate = o_ssm + SSM_IN
    kr0 = MLA_Q_RANK + MLA_KV_RANK
    w_cat = jnp.concatenate([
        w[:, :kr0], zeros(MLA_NOPE), w[:, kr0:MLA_IN], zeros(LANES - MLA_QK_DIM),
        w[:, o_fox:o_fox + 3 * FOX_WIDTH],
        w[:, o_ssm:o_ssm + SSM_INNER + SSM_XBC],
        w[:, o_gate:],
    ], axis=1).astype(BF16)
    assert w_cat.shape[1] == W_CAT
    f0 = o_fox + 3 * FOX_WIDTH
    d0 = o_ssm + SSM_INNER + SSM_XBC
    w_small = jnp.concatenate([
        w[:, f0:f0 + FOX_HEADS], zeros(LANES - FOX_HEADS),
        w[:, d0:d0 + SSM_HEADS], zeros(LANES - SSM_HEADS),
    ], axis=1).astype(BF16)
    return w_cat, w_small


def _fox_placement():
    rows = jnp.arange(3 * FOX_HEADS)
    piece = rows // FOX_HEADS
    head = rows % FOX_HEADS
    pq = jnp.zeros((LANES, FOX_HEADS * HEAD_PAD), F32).at[rows, head * HEAD_PAD + AUG0 + piece].set(1.0)
    pk = jnp.zeros((LANES, FOX_HEADS * HEAD_PAD), F32).at[rows, head * HEAD_PAD + AUG0 + 3 + piece].set(-1.0)
    oneq = jnp.zeros((1, LANES), F32).at[0, AUG0 + 3:AUG0 + 6].set(1.0)
    onek = jnp.zeros((1, LANES), F32).at[0, AUG0:AUG0 + 3].set(1.0)
    return pq.astype(BF16), pk.astype(BF16), oneq, onek


def _head_expand():
    h = jnp.arange(SSM_HEADS)
    e = jnp.zeros((LANES, SSM_INNER), F32)
    cols = h[:, None] * SSM_HEAD_DIM + jnp.arange(SSM_HEAD_DIM)[None, :]
    return e.at[h[:, None], cols].set(1.0).astype(BF16)


def kernel(x, positions, norm_mix_g, w_in, b_gate, mla_q_norm_g, mla_w_uq, mla_kv_norm_g, mla_w_ukv,
           mla_q_gain, mla_k_gain, fox_q_gain, fox_k_gain, fox_b_f, ssm_conv_w, ssm_conv_b, ssm_dt_bias,
           ssm_A_log, ssm_D, ssm_norm_g, w_br_mla, w_br_fox, w_br_ssm, w_out, norm_ffn_g, ffn_w_up,
           ffn_conv_w, ffn_conv_b, ffn_w_down):
    B, S, D = x.shape
    T = B * S
    depth = w_in.shape[0]
    tile = min(256, S)

    pos = positions.reshape(T, 1).astype(F32)
    cos, sa, sb = _rope_tables(pos, tile)
    pq, pk, oneq, onek = _fox_placement()
    e01 = _head_expand()
    row = lambda v: v.reshape(1, -1).astype(F32)

    x2 = x.reshape(T, D).astype(F32)
    for l in range(depth):
        w_cat, w_small = _pack_w_in(w_in[l])
        u_mla, u_fox, u_z, u_xbc, u_gate, u_small = _inproj(x2, row(norm_mix_g[l]), w_cat, w_small, tile)

        wq = jnp.pad(mla_w_uq[l].reshape(MLA_Q_RANK, MLA_HEADS, MLA_QK_DIM),
                     ((0, 0), (0, 0), (0, HEAD_PAD - MLA_QK_DIM))).reshape(MLA_Q_RANK, -1).astype(BF16)
        wkv = mla_w_ukv[l].reshape(MLA_KV_RANK, MLA_HEADS, MLA_NOPE + MLA_V)
        wk = jnp.pad(wkv[:, :, :MLA_NOPE], ((0, 0), (0, 0), (0, HEAD_PAD - MLA_NOPE))
                     ).reshape(MLA_KV_RANK, -1).astype(BF16)
        wv = wkv[:, :, MLA_NOPE:].reshape(MLA_KV_RANK, -1).astype(BF16)
        q, k, vt = _mla_prep(u_mla, cos, sa, sb, row(mla_q_norm_g[l]), wq, row(mla_kv_norm_g[l]), wk, wv,
                             _pad_lanes(mla_q_gain[l]), _pad_lanes(mla_k_gain[l]), B, S, tile)
        y_a = _attention(q, k, vt, tile, True).reshape(T, -1)

        qf, kf, vtf = _fox_prep(u_fox, u_small, _pad_lanes(fox_b_f[l]), _pad_lanes(fox_q_gain[l]),
                                _pad_lanes(fox_k_gain[l]), pq, pk, oneq, onek, B, S, tile)
        y_b = _attention(qf, kf, vtf, tile, False).reshape(T, -1)

        dx = jnp.repeat(ssm_D[l].astype(F32), SSM_HEAD_DIM).reshape(1, -1)
        y_c = _ssm(u_z, u_xbc, u_small, ssm_conv_w[l].astype(F32), row(ssm_conv_b[l]),
                   _pad_lanes(ssm_dt_bias[l]), _pad_lanes(ssm_A_log[l]), dx, row(ssm_norm_g[l]), e01,
                   B, S, tile)

        x2 = _merge(x2, y_a, y_b, y_c, u_gate, row(b_gate[l]), w_br_mla[l].astype(BF16),
                    w_br_fox[l].astype(BF16), w_br_ssm[l].astype(BF16), w_out[l].astype(BF16), tile)
        x2 = _ffn(x2, row(norm_ffn_g[l]), ffn_w_up[l].astype(BF16), ffn_conv_w[l].astype(F32),
                  row(ffn_conv_b[l]), ffn_w_down[l].astype(BF16), S, tile)
    return x2.reshape(B, S, D).astype(x.dtype)
```

```python
import functools
import math

import jax
import jax.numpy as jnp
from jax import lax
from jax.experimental import pallas as pl
from jax.experimental.pallas import tpu as pltpu

F32 = jnp.float32
BF16 = jnp.bfloat16

D_MODEL = 1024
NORM_EPS = 1e-6
CHUNK = 64

MLA_HEADS = 8
MLA_Q_RANK = 384
MLA_KV_RANK = 256
MLA_NOPE = 64
MLA_ROPE = 32
MLA_QK_DIM = MLA_NOPE + MLA_ROPE
MLA_V = 64
ROPE_THETA = 10000.0

FOX_HEADS = 8
FOX_HEAD_DIM = 64
FOX_WIDTH = FOX_HEADS * FOX_HEAD_DIM

SSM_HEADS = 16
SSM_HEAD_DIM = 64
SSM_INNER = SSM_HEADS * SSM_HEAD_DIM
SSM_GROUPS = 2
SSM_STATE = 128
SSM_CONV = 4
SSM_XBC = SSM_INNER + 2 * SSM_GROUPS * SSM_STATE
SSM_GROUP_WIDTH = SSM_INNER // SSM_GROUPS

D_FF = 2816
FFN_CONV = 3

MLA_IN = MLA_Q_RANK + MLA_KV_RANK + MLA_ROPE
FOX_IN = 3 * FOX_WIDTH + FOX_HEADS
SSM_IN = SSM_INNER + SSM_XBC + SSM_HEADS

LANES = 128
SUBLANES = 8
HEAD_PAD = LANES
LOG2E = 1.4426950408889634
NEG_BIG = -1e30
VMEM_LIMIT = 52 * 1024 * 1024

MLA_SEG = MLA_Q_RANK + MLA_KV_RANK + LANES
SEG_MLA = (0, MLA_SEG)
SEG_FOX = (SEG_MLA[1], SEG_MLA[1] + 3 * FOX_WIDTH)
SEG_Z = (SEG_FOX[1], SEG_FOX[1] + SSM_INNER)
SEG_XBC = (SEG_Z[1], SEG_Z[1] + SSM_XBC)
SEG_GATE = (SEG_XBC[1], SEG_XBC[1] + 3 * D_MODEL)
W_CAT = SEG_GATE[1]
SMALL_W = 2 * LANES

AUG0 = FOX_HEAD_DIM


def _cparams(n_grid):
    return pltpu.CompilerParams(dimension_semantics=("arbitrary",) * n_grid,
                                vmem_limit_bytes=VMEM_LIMIT)


def _vmem_full():
    return pl.BlockSpec(memory_space=pltpu.VMEM)


def _dot(a, b):
    return jnp.dot(a, b, preferred_element_type=F32)


def _dot_nt(a, b):
    return lax.dot_general(a, b, (((1,), (1,)), ((), ())), preferred_element_type=F32)


def _split3(x):
    hi = x.astype(BF16)
    r = x - hi.astype(F32)
    mid = r.astype(BF16)
    lo = (r - mid.astype(F32)).astype(BF16)
    return hi, mid, lo


def _dot3_left(m01, x):
    hi, mid, lo = _split3(x)
    return _dot(m01, hi) + _dot(m01, mid) + _dot(m01, lo)


def _dot3_right(x, m01):
    hi, mid, lo = _split3(x)
    return _dot(hi, m01) + _dot(mid, m01) + _dot(lo, m01)


def _sigmoid(x):
    return 1.0 / (1.0 + jnp.exp(-x))


def _silu(x):
    return x * _sigmoid(x)


def _softplus(x):
    return jnp.maximum(x, 0.0) + jnp.log(1.0 + jnp.exp(-jnp.abs(x)))


def _tril_bf16(n):
    r = lax.broadcasted_iota(jnp.int32, (n, n), 0)
    c = lax.broadcasted_iota(jnp.int32, (n, n), 1)
    return jnp.where(r >= c, 1.0, 0.0).astype(BF16)


def _rope_kernel(pos_ref, inv_ref, cos_ref, sa_ref, sb_ref):
    ang = pos_ref[...] * inv_ref[...]
    lane = lax.broadcasted_iota(jnp.int32, ang.shape, 1)
    c = jnp.cos(ang)
    s = jnp.sin(ang)
    half = MLA_ROPE // 2
    first = (lane >= MLA_NOPE) & (lane < MLA_NOPE + half)
    second = (lane >= MLA_NOPE + half) & (lane < MLA_QK_DIM)
    cos_ref[...] = jnp.where(lane < MLA_NOPE, 1.0, jnp.where(first | second, c, 0.0))
    sa_ref[...] = jnp.where(first, -s, 0.0)
    sb_ref[...] = jnp.where(second, s, 0.0)


def _rope_tables(pos, ts):
    T = pos.shape[0]
    half = MLA_ROPE // 2
    inv = 1.0 / (ROPE_THETA ** (jnp.arange(0, MLA_ROPE, 2, dtype=F32) / MLA_ROPE))
    inv_lane = jnp.zeros((1, LANES), F32)
    inv_lane = inv_lane.at[0, MLA_NOPE:MLA_NOPE + half].set(inv)
    inv_lane = inv_lane.at[0, MLA_NOPE + half:MLA_QK_DIM].set(inv)
    tab = jax.ShapeDtypeStruct((T, LANES), F32)
    spec = pl.BlockSpec((ts, LANES), lambda i: (i, 0))
    return pl.pallas_call(
        _rope_kernel,
        grid=(T // ts,),
        in_specs=[pl.BlockSpec((ts, 1), lambda i: (i, 0)), pl.BlockSpec((1, LANES), lambda i: (0, 0))],
        out_specs=[spec, spec, spec],
        out_shape=[tab, tab, tab],
        compiler_params=_cparams(1),
        name="rope_tables",
    )(pos, inv_lane)


def _rope(x, cos, sa, sb):
    half = MLA_ROPE // 2
    return x * cos + pltpu.roll(x, LANES - half, axis=1) * sa + pltpu.roll(x, half, axis=1) * sb


IN_CH = 512


def _inproj_kernel(x_ref, g_ref, w_ref, wsm_ref, o_mla, o_fox, o_z, o_xbc, o_gate, o_small):
    x = x_ref[...]
    hn = (x * lax.rsqrt(jnp.mean(x * x, axis=-1, keepdims=True) + NORM_EPS) * g_ref[...]).astype(BF16)
    for o_ref, (c0, c1) in ((o_mla, SEG_MLA), (o_fox, SEG_FOX), (o_z, SEG_Z), (o_xbc, SEG_XBC),
                            (o_gate, SEG_GATE)):
        for cc in range(c0, c1, IN_CH):
            w = min(IN_CH, c1 - cc)
            o_ref[:, cc - c0:cc - c0 + w] = _dot(hn, w_ref[:, cc:cc + w]).astype(o_ref.dtype)
    o_small[...] = _dot(hn, wsm_ref[...])


def _inproj(x2, g, w_cat, w_small, tm):
    T = x2.shape[0]
    widths = [s[1] - s[0] for s in (SEG_MLA, SEG_FOX, SEG_Z, SEG_XBC, SEG_GATE)]
    out_shape = [jax.ShapeDtypeStruct((T, w), BF16) for w in widths]
    out_shape.append(jax.ShapeDtypeStruct((T, SMALL_W), F32))
    out_specs = [pl.BlockSpec((tm, w), lambda i: (i, 0)) for w in widths]
    out_specs.append(pl.BlockSpec((tm, SMALL_W), lambda i: (i, 0)))
    return pl.pallas_call(
        _inproj_kernel,
        grid=(T // tm,),
        in_specs=[pl.BlockSpec((tm, D_MODEL), lambda i: (i, 0)),
                  pl.BlockSpec((1, D_MODEL), lambda i: (0, 0)),
                  _vmem_full(), _vmem_full()],
        out_specs=out_specs,
        out_shape=out_shape,
        compiler_params=_cparams(1),
        name="inproj",
    )(x2, g, w_cat, w_small)


def _track_key_norm(kmax_out, h, k_rows):
    kf = k_rows.astype(F32)
    block_max = jnp.max(jnp.sum(kf * kf, axis=-1, keepdims=True), axis=0, keepdims=True)
    kmax_out[0, h:h + 1, :] = jnp.maximum(kmax_out[0, h:h + 1, :], block_max)


def _mla_prep_kernel(u_ref, cos_ref, sa_ref, sb_ref, gq_ref, wq_ref, gkv_ref, wk_ref, wv_ref,
                     qg_ref, kg_ref, q_out, k_out, vt_out, kmax_out):
    @pl.when(pl.program_id(1) == 0)
    def _():
        kmax_out[...] = jnp.zeros_like(kmax_out)

    u = u_ref[...].astype(F32)
    cos = cos_ref[...]
    sa = sa_ref[...]
    sb = sb_ref[...]
    cq = u[:, :MLA_Q_RANK]
    ckv = u[:, MLA_Q_RANK:MLA_Q_RANK + MLA_KV_RANK]
    kr = u[:, MLA_Q_RANK + MLA_KV_RANK:]
    cqn = (cq * lax.rsqrt(jnp.mean(cq * cq, axis=-1, keepdims=True) + NORM_EPS) * gq_ref[...]).astype(BF16)
    ckvn = (ckv * lax.rsqrt(jnp.mean(ckv * ckv, axis=-1, keepdims=True) + NORM_EPS) * gkv_ref[...]).astype(BF16)
    q_all = _dot(cqn, wq_ref[...])
    k_all = _dot(ckvn, wk_ref[...])
    v_all = _dot(ckvn, wv_ref[...])
    kr_ss = jnp.sum(kr * kr, axis=-1, keepdims=True)
    q_scale = (MLA_QK_DIM ** -0.5) * LOG2E
    for h in range(MLA_HEADS):
        qh = q_all[:, h * HEAD_PAD:(h + 1) * HEAD_PAD]
        ms = jnp.sum(qh * qh, axis=-1, keepdims=True) * (1.0 / MLA_QK_DIM)
        qh = qh * lax.rsqrt(ms + NORM_EPS) * qg_ref[...]
        q_out[0, h] = (_rope(qh, cos, sa, sb) * q_scale).astype(q_out.dtype)
        kn = k_all[:, h * HEAD_PAD:(h + 1) * HEAD_PAD]
        ms = (jnp.sum(kn * kn, axis=-1, keepdims=True) + kr_ss) * (1.0 / MLA_QK_DIM)
        kh = (kn + kr) * lax.rsqrt(ms + NORM_EPS) * kg_ref[...]
        kb = _rope(kh, cos, sa, sb).astype(k_out.dtype)
        k_out[0, h] = kb
        _track_key_norm(kmax_out, h, kb)
    vt_out[0, :, 0] = _vt_tile(v_all, MLA_HEADS)


def _mla_prep(u_mla, cos, sa, sb, gq, wq, gkv, wk, wv, qg, kg, B, S, ts):
    nb = S // ts
    H = MLA_HEADS
    tok = lambda b, i: (b * nb + i, 0)
    const = lambda b, i: (0, 0)
    return pl.pallas_call(
        _mla_prep_kernel,
        grid=(B, nb),
        in_specs=[pl.BlockSpec((ts, MLA_SEG), tok),
                  pl.BlockSpec((ts, LANES), tok), pl.BlockSpec((ts, LANES), tok), pl.BlockSpec((ts, LANES), tok),
                  pl.BlockSpec((1, MLA_Q_RANK), const), _vmem_full(),
                  pl.BlockSpec((1, MLA_KV_RANK), const), _vmem_full(), _vmem_full(),
                  pl.BlockSpec((1, LANES), const), pl.BlockSpec((1, LANES), const)],
        out_specs=[pl.BlockSpec((1, H, ts, HEAD_PAD), lambda b, i: (b, 0, i, 0)),
                   pl.BlockSpec((1, H, ts, HEAD_PAD), lambda b, i: (b, 0, i, 0)),
                   pl.BlockSpec((1, H, 1, V_ROWS, ts), lambda b, i: (b, 0, i, 0, 0)),
                   pl.BlockSpec((1, H, LANES), lambda b, i: (b, 0, 0))],
        out_shape=[jax.ShapeDtypeStruct((B, H, S, HEAD_PAD), BF16),
                   jax.ShapeDtypeStruct((B, H, S, HEAD_PAD), BF16),
                   jax.ShapeDtypeStruct((B, H, nb, V_ROWS, ts), BF16),
                   jax.ShapeDtypeStruct((B, H, LANES), F32)],
        compiler_params=_cparams(2),
        name="mla_prep",
    )(u_mla, cos, sa, sb, gq, wq, gkv, wk, wv, qg, kg)


def _fox_prep_kernel(u_ref, f_ref, bf_ref, qg_ref, kg_ref, pq_ref, pk_ref, oneq_ref, onek_ref,
                     q_out, k_out, vt_out, kmax_out, carry):
    ts = u_ref.shape[0]

    @pl.when(pl.program_id(1) == 0)
    def _():
        carry[...] = jnp.zeros_like(carry)
        kmax_out[...] = jnp.zeros_like(kmax_out)

    lane = lax.broadcasted_iota(jnp.int32, (ts, LANES), 1)
    f = f_ref[...] + bf_ref[...]
    log_f = jnp.minimum(f, 0.0) - jnp.log(1.0 + jnp.exp(-jnp.abs(f)))
    log_f = jnp.where(lane < FOX_HEADS, log_f * LOG2E, 0.0)
    cum = carry[...] + _dot3_left(_tril_bf16(ts), log_f)
    carry[...] = cum[ts - 1:ts, :]
    hi, mid, lo = _split3(cum)
    packed = (hi.astype(F32) + pltpu.roll(mid.astype(F32), FOX_HEADS, axis=1)
              + pltpu.roll(lo.astype(F32), 2 * FOX_HEADS, axis=1)).astype(BF16)
    aug_q = _dot(packed, pq_ref[...])
    aug_k = _dot(packed, pk_ref[...])

    u = u_ref[...].astype(F32)
    q_scale = (FOX_HEAD_DIM ** -0.5) * LOG2E
    head_lanes = lane < FOX_HEAD_DIM
    for h in range(FOX_HEADS):
        t0 = (h // 2) * LANES
        for src, gain, aug, ones, scale, out in (
                (0, qg_ref, aug_q, oneq_ref, q_scale, q_out),
                (FOX_WIDTH, kg_ref, aug_k, onek_ref, 1.0, k_out)):
            blk = u[:, src + t0:src + t0 + LANES]
            if h % 2 == 1:
                blk = pltpu.roll(blk, FOX_HEAD_DIM, axis=1)
            blk = jnp.where(head_lanes, blk, 0.0)
            ms = jnp.sum(blk * blk, axis=-1, keepdims=True) * (1.0 / FOX_HEAD_DIM)
            blk = blk * lax.rsqrt(ms + NORM_EPS) * (gain[...] * scale)
            if out is k_out:
                _track_key_norm(kmax_out, h, blk.astype(out.dtype))
            blk = blk + aug[:, h * HEAD_PAD:(h + 1) * HEAD_PAD] + ones[...]
            out[0, h] = blk.astype(out.dtype)
    vt_out[0, :, 0] = _vt_tile(u[:, 2 * FOX_WIDTH:], FOX_HEADS)


def _fox_prep(u_fox, u_small, bf, qg, kg, pq, pk, oneq, onek, B, S, ts):
    nb = S // ts
    H = FOX_HEADS
    tok = lambda b, i: (b * nb + i, 0)
    const = lambda b, i: (0, 0)
    return pl.pallas_call(
        _fox_prep_kernel,
        grid=(B, nb),
        in_specs=[pl.BlockSpec((ts, 3 * FOX_WIDTH), tok),
                  pl.BlockSpec((ts, LANES), tok),
                  pl.BlockSpec((1, LANES), const), pl.BlockSpec((1, LANES), const),
                  pl.BlockSpec((1, LANES), const), _vmem_full(), _vmem_full(),
                  pl.BlockSpec((1, LANES), const), pl.BlockSpec((1, LANES), const)],
        out_specs=[pl.BlockSpec((1, H, ts, HEAD_PAD), lambda b, i: (b, 0, i, 0)),
                   pl.BlockSpec((1, H, ts, HEAD_PAD), lambda b, i: (b, 0, i, 0)),
                   pl.BlockSpec((1, H, 1, V_ROWS, ts), lambda b, i: (b, 0, i, 0, 0)),
                   pl.BlockSpec((1, H, LANES), lambda b, i: (b, 0, 0))],
        out_shape=[jax.ShapeDtypeStruct((B, H, S, HEAD_PAD), BF16),
                   jax.ShapeDtypeStruct((B, H, S, HEAD_PAD), BF16),
                   jax.ShapeDtypeStruct((B, H, nb, V_ROWS, ts), BF16),
                   jax.ShapeDtypeStruct((B, H, LANES), F32)],
        scratch_shapes=[pltpu.VMEM((1, LANES), F32)],
        compiler_params=_cparams(2),
        name="fox_prep",
    )(u_fox, u_small, bf, qg, kg, pq, pk, oneq, onek)


ATTN_TILE = 512
V_DIM = 64
V_ROWS = 80


def _vt_tile(v_all, heads):
    ts = v_all.shape[0]
    vt = v_all.T.reshape(heads, V_DIM, ts)
    row = lax.broadcasted_iota(jnp.int32, (heads, V_ROWS - V_DIM, ts), 1)
    extra = jnp.where(row == 0, 1.0, 0.0)
    return jnp.concatenate([vt, extra], axis=1).astype(BF16)


HEADS_PER_STEP = 4
BOUND_SLACK = 1.02
MAX_SAFE_BOUND = 60.0


def _attn_kernel(q_ref, k_ref, vt_ref, kmax_ref, o_ref, *, tq, chunk_causal, score_lanes):
    hg = pl.program_id(1)
    i = pl.program_id(2)
    hs = q_ref.shape[1]
    dvp = vt_ref.shape[3]
    dv = o_ref.shape[2] // hs

    def scores(hh, j):
        kj = k_ref[0, hh, pl.ds(pl.multiple_of(j * tq, tq), tq), :]
        return _dot_nt(kj, q_ref[0, hh])

    def diag_mask():
        r = lax.broadcasted_iota(jnp.int32, (tq, tq), 0)
        c = lax.broadcasted_iota(jnp.int32, (tq, tq), 1)
        shift = CHUNK.bit_length() - 1
        return ((r >> shift) <= (c >> shift)) if chunk_causal else (r <= c)

    def finish(accs):
        o = jnp.concatenate([acc[:dv] * (1.0 / acc[dv:dv + 1]) for acc in accs], axis=0)
        o_ref[0] = o.T.astype(o_ref.dtype)

    lane = lax.broadcasted_iota(jnp.int32, (SUBLANES, HEAD_PAD), 1)
    sel = jnp.where(lane < score_lanes, 1.0, 0.0).astype(BF16)
    bounds = []
    for hh in range(hs):
        qf = q_ref[0, hh].astype(F32)
        qn2 = _dot_nt(sel, (qf * qf).astype(BF16))[0:1]
        kn2 = kmax_ref[0, pl.ds(hg * hs + hh, 1), :][:, 0:1]
        bounds.append(jnp.sqrt(qn2 * kn2) * BOUND_SLACK)
    worst = jnp.max(jnp.concatenate(bounds, axis=0))
    safe = worst < MAX_SAFE_BOUND

    @pl.when(safe)
    def _():
        def tile(j, accs, diagonal):
            ok = diag_mask() if diagonal else None
            new = []
            s_next = scores(0, j)
            for hh in range(hs):
                s = s_next
                if hh + 1 < hs:
                    s_next = scores(hh + 1, j)
                if diagonal:
                    s = jnp.where(ok, s, NEG_BIG)
                p = jnp.exp2(s - bounds[hh]).astype(BF16)
                new.append(accs[hh] + _dot(vt_ref[0, hh, j], p))
            return tuple(new)

        init = tuple(jnp.zeros((dvp, tq), F32) for _ in range(hs))
        accs = lax.fori_loop(0, i, lambda j, c: tile(j, c, False), init)
        finish(tile(i, accs, True))

    @pl.when(jnp.logical_not(safe))
    def _():
        def tile(j, carry, diagonal):
            ok = diag_mask() if diagonal else None
            new = []
            s_next = scores(0, j)
            for hh in range(hs):
                s = s_next
                if hh + 1 < hs:
                    s_next = scores(hh + 1, j)
                m, acc = carry[hh]
                if diagonal:
                    s = jnp.where(ok, s, NEG_BIG)
                m_new = jnp.maximum(m, jnp.max(s, axis=0, keepdims=True))
                alpha = jnp.exp2(m - m_new)
                p = jnp.exp2(s - m_new).astype(BF16)
                new.append((m_new, alpha * acc + _dot(vt_ref[0, hh, j], p)))
            return tuple(new)

        init = tuple((jnp.full((1, tq), NEG_BIG, F32), jnp.zeros((dvp, tq), F32)) for _ in range(hs))
        carry = lax.fori_loop(0, i, lambda j, c: tile(j, c, False), init)
        finish([acc for _, acc in tile(i, carry, True)])


def _attention(q, k, vt, kmax, tq, chunk_causal, score_lanes):
    B, H, S, _ = q.shape
    dvp = vt.shape[3]
    dv = V_DIM
    hs = HEADS_PER_STEP
    assert vt.shape[4] == tq and dvp == V_ROWS and (hs * dv) % LANES == 0 and H % hs == 0
    nq = S // tq
    return pl.pallas_call(
        functools.partial(_attn_kernel, tq=tq, chunk_causal=chunk_causal, score_lanes=score_lanes),
        grid=(B, H // hs, nq),
        in_specs=[pl.BlockSpec((1, hs, tq, HEAD_PAD), lambda b, h, i: (b, h, i, 0)),
                  pl.BlockSpec((1, hs, S, HEAD_PAD), lambda b, h, i: (b, h, 0, 0)),
                  pl.BlockSpec((1, hs, nq, dvp, tq), lambda b, h, i: (b, h, 0, 0, 0)),
                  pl.BlockSpec((1, H, LANES), lambda b, h, i: (b, 0, 0))],
        out_specs=pl.BlockSpec((1, tq, hs * dv), lambda b, h, i: (b, i, h)),
        out_shape=jax.ShapeDtypeStruct((B, S, H * dv), BF16),
        compiler_params=_cparams(3),
        name="attn_mla" if chunk_causal else "attn_fox",
    )(q, k, vt, kmax)


def _ssm_kernel(z_ref, xbc_ref, dt_ref, cw_ref, cb_ref, dtb_ref, alog_ref, dx_ref, ng_ref, e_ref,
                o_ref, xpad, hst, *, lc):
    pad = SUBLANES

    @pl.when(pl.program_id(1) == 0)
    def _():
        xpad[0:pad, :] = jnp.zeros((pad, SSM_XBC), F32)
        hst[...] = jnp.zeros_like(hst)

    xpad[pad:pad + lc, :] = xbc_ref[...].astype(F32)
    conv = cb_ref[...]
    for kk in range(SSM_CONV):
        off = pad - (SSM_CONV - 1) + kk
        conv = conv + cw_ref[kk:kk + 1, :] * xpad[off:off + lc, :]
    xpad[0:pad, :] = xpad[lc:lc + pad, :]
    xc = _silu(conv)
    xs = xc[:, :SSM_INNER]
    bm = xc[:, SSM_INNER:SSM_INNER + SSM_GROUPS * SSM_STATE]
    cm = xc[:, SSM_INNER + SSM_GROUPS * SSM_STATE:]

    lane = lax.broadcasted_iota(jnp.int32, (lc, LANES), 1)
    dt = jnp.where(lane < SSM_HEADS, _softplus(dt_ref[...] + dtb_ref[...]), 0.0)
    a_neg = -jnp.exp(alog_ref[...])
    a_cum = _dot3_left(_tril_bf16(lc), dt * a_neg)
    ea = jnp.exp(a_cum)
    dte = jnp.exp(a_cum[lc - 1:lc, :] - a_cum)
    e01 = e_ref[...]
    dt_x = _dot3_right(dt, e01)
    ea_x = _dot3_right(ea, e01)
    dte_x = _dot3_right(dte, e01)
    xdt = xs * dt_x
    a_t = a_cum.T

    row = lax.broadcasted_iota(jnp.int32, (lc, lc), 0)
    col = lax.broadcasted_iota(jnp.int32, (lc, lc), 1)
    causal = row >= col
    lane_first = lane < SSM_HEAD_DIM
    heads_per_group = SSM_HEADS // SSM_GROUPS
    y_blocks = []
    for g in range(SSM_GROUPS):
        bg = bm[:, g * SSM_STATE:(g + 1) * SSM_STATE]
        cg = cm[:, g * SSM_STATE:(g + 1) * SSM_STATE].astype(BF16)
        cb = _dot_nt(cg, bg.astype(BF16))
        gs = slice(g * SSM_GROUP_WIDTH, (g + 1) * SSM_GROUP_WIDTH)
        h_prev = hst[g]
        y_off = _dot(cg, h_prev.astype(BF16)) * ea_x[:, gs]
        xw = (xdt[:, gs] * dte_x[:, gs]).astype(BF16)
        s_new = _dot(bg.T.astype(BF16), xw)
        hst[g] = h_prev * ea_x[lc - 1:lc, gs] + s_new
        for hp in range(heads_per_group // 2):
            t0 = g * SSM_GROUP_WIDTH + hp * LANES
            x_pair = xdt[:, t0:t0 + LANES]
            acc = None
            for e in range(2):
                h = g * heads_per_group + hp * 2 + e
                seg = a_cum[:, h:h + 1] - a_t[h:h + 1, :]
                decay = jnp.exp(jnp.where(causal, seg, NEG_BIG))
                gmat = (cb * decay).astype(BF16)
                xh = jnp.where(lane_first if e == 0 else ~lane_first, x_pair, 0.0).astype(BF16)
                part = _dot(gmat, xh)
                acc = part if acc is None else acc + part
            y_blocks.append(acc + y_off[:, hp * LANES:(hp + 1) * LANES])
    y = jnp.concatenate(y_blocks, axis=1) + xs * dx_ref[...]
    zf = z_ref[...].astype(F32)
    yg = y * _silu(zf)
    o_ref[...] = (yg * lax.rsqrt(jnp.mean(yg * yg, axis=-1, keepdims=True) + NORM_EPS)
                  * ng_ref[...]).astype(o_ref.dtype)


def _ssm(u_z, u_xbc, u_small, cw, cb, dtb, alog, dx, ng, e01, B, S, lc):
    nb = S // lc
    tok = lambda b, i: (b * nb + i, 0)
    const = lambda b, i: (0, 0)
    return pl.pallas_call(
        functools.partial(_ssm_kernel, lc=lc),
        grid=(B, nb),
        in_specs=[pl.BlockSpec((lc, SSM_INNER), tok),
                  pl.BlockSpec((lc, SSM_XBC), tok),
                  pl.BlockSpec((lc, LANES), lambda b, i: (b * nb + i, 1)),
                  pl.BlockSpec((SSM_CONV, SSM_XBC), const), pl.BlockSpec((1, SSM_XBC), const),
                  pl.BlockSpec((1, LANES), const), pl.BlockSpec((1, LANES), const),
                  pl.BlockSpec((1, SSM_INNER), const), pl.BlockSpec((1, SSM_INNER), const),
                  _vmem_full()],
        out_specs=pl.BlockSpec((lc, SSM_INNER), tok),
        out_shape=jax.ShapeDtypeStruct((B * S, SSM_INNER), BF16),
        scratch_shapes=[pltpu.VMEM((lc + SUBLANES, SSM_XBC), F32),
                        pltpu.VMEM((SSM_GROUPS, SSM_STATE, SSM_GROUP_WIDTH), F32)],
        compiler_params=_cparams(2),
        name="ssm",
    )(u_z, u_xbc, u_small, cw, cb, dtb, alog, dx, ng, e01)


def _merge_kernel(x_ref, ya_ref, yb_ref, yc_ref, ug_ref, bg_ref, wa_ref, wb_ref, wc_ref, wo_ref, o_ref):
    ys = (_dot(ya_ref[...], wa_ref[...]), _dot(yb_ref[...], wb_ref[...]), _dot(yc_ref[...], wc_ref[...]))
    merged = None
    for n, y in enumerate(ys):
        cs = slice(n * D_MODEL, (n + 1) * D_MODEL)
        gate = _sigmoid(ug_ref[:, cs].astype(F32) + bg_ref[:, cs])
        merged = gate * y if merged is None else merged + gate * y
    o_ref[...] = x_ref[...] + _dot(merged.astype(BF16), wo_ref[...])


def _merge(x2, ya, yb, yc, ug, bg, wa, wb, wc, wo, tm):
    T = x2.shape[0]
    row = lambda w: pl.BlockSpec((tm, w), lambda i: (i, 0))
    return pl.pallas_call(
        _merge_kernel,
        grid=(T // tm,),
        in_specs=[row(D_MODEL), row(MLA_HEADS * MLA_V), row(FOX_WIDTH), row(SSM_INNER), row(3 * D_MODEL),
                  pl.BlockSpec((1, 3 * D_MODEL), lambda i: (0, 0)),
                  _vmem_full(), _vmem_full(), _vmem_full(), _vmem_full()],
        out_specs=row(D_MODEL),
        out_shape=jax.ShapeDtypeStruct((T, D_MODEL), F32),
        compiler_params=_cparams(1),
        name="merge",
    )(x2, ya, yb, yc, ug, bg, wa, wb, wc, wo)


FF_CH = 256


def _ffn_kernel(x_ref, g_ref, wup_ref, cw_ref, cb_ref, wdn_ref, o_ref, abuf, hbuf, *, tm, blocks_per_seq):
    pad = SUBLANES

    @pl.when(pl.program_id(0) % blocks_per_seq == 0)
    def _():
        abuf[0:pad, :] = jnp.zeros((pad, 2 * D_FF), F32)

    x = x_ref[...]
    hn = (x * lax.rsqrt(jnp.mean(x * x, axis=-1, keepdims=True) + NORM_EPS) * g_ref[...]).astype(BF16)
    for c0 in range(0, D_FF, FF_CH):
        halves = []
        for base in (c0, D_FF + c0):
            cs = slice(base, base + FF_CH)
            abuf[pad:pad + tm, cs] = _dot(hn, wup_ref[:, cs])
            conv = cb_ref[:, cs]
            for kk in range(FFN_CONV):
                off = pad - (FFN_CONV - 1) + kk
                conv = conv + cw_ref[kk:kk + 1, cs] * abuf[off:off + tm, cs]
            abuf[0:pad, cs] = abuf[tm:tm + pad, cs]
            halves.append(conv)
        hbuf[:, c0:c0 + FF_CH] = (_silu(halves[0]) * halves[1]).astype(BF16)
    o_ref[...] = x + _dot(hbuf[...], wdn_ref[...])


def _ffn(x2, g, wup, cw, cb, wdn, S, tm):
    T = x2.shape[0]
    return pl.pallas_call(
        functools.partial(_ffn_kernel, tm=tm, blocks_per_seq=S // tm),
        grid=(T // tm,),
        in_specs=[pl.BlockSpec((tm, D_MODEL), lambda i: (i, 0)),
                  pl.BlockSpec((1, D_MODEL), lambda i: (0, 0)),
                  _vmem_full(),
                  pl.BlockSpec((FFN_CONV, 2 * D_FF), lambda i: (0, 0)),
                  pl.BlockSpec((1, 2 * D_FF), lambda i: (0, 0)),
                  _vmem_full()],
        out_specs=pl.BlockSpec((tm, D_MODEL), lambda i: (i, 0)),
        out_shape=jax.ShapeDtypeStruct((T, D_MODEL), F32),
        scratch_shapes=[pltpu.VMEM((tm + SUBLANES, 2 * D_FF), F32),
                        pltpu.VMEM((tm, D_FF), BF16)],
        compiler_params=_cparams(1),
        name="ffn",
    )(x2, g, wup, cw, cb, wdn)


def _pad_lanes(v, width=LANES, offset=0):
    out = jnp.zeros((1, width), F32)
    return out.at[0, offset:offset + v.shape[0]].set(v.astype(F32))


def _pack_w_in(w):
    zeros = lambda n: jnp.zeros((D_MODEL, n), w.dtype)
    o_fox = MLA_IN
    o_ssm = MLA_IN + FOX_IN
    o_gate = o_ssm + SSM_IN
    kr0 = MLA_Q_RANK + MLA_KV_RANK
    w_cat = jnp.concatenate([
        w[:, :kr0], zeros(MLA_NOPE), w[:, kr0:MLA_IN], zeros(LANES - MLA_QK_DIM),
        w[:, o_fox:o_fox + 3 * FOX_WIDTH],
        w[:, o_ssm:o_ssm + SSM_INNER + SSM_XBC],
        w[:, o_gate:],
    ], axis=1).astype(BF16)
    assert w_cat.shape[1] == W_CAT
    f0 = o_fox + 3 * FOX_WIDTH
    d0 = o_ssm + SSM_INNER + SSM_XBC
    w_small = jnp.concatenate([
        w[:, f0:f0 + FOX_HEADS], zeros(LANES - FOX_HEADS),
        w[:, d0:d0 + SSM_HEADS], zeros(LANES - SSM_HEADS),
    ], axis=1).astype(BF16)
    return w_cat, w_small


def _fox_placement():
    rows = jnp.arange(3 * FOX_HEADS)
    piece = rows // FOX_HEADS
    head = rows % FOX_HEADS
    pq = jnp.zeros((LANES, FOX_HEADS * HEAD_PAD), F32).at[rows, head * HEAD_PAD + AUG0 + piece].set(1.0)
    pk = jnp.zeros((LANES, FOX_HEADS * HEAD_PAD), F32).at[rows, head * HEAD_PAD + AUG0 + 3 + piece].set(-1.0)
    oneq = jnp.zeros((1, LANES), F32).at[0, AUG0 + 3:AUG0 + 6].set(1.0)
    onek = jnp.zeros((1, LANES), F32).at[0, AUG0:AUG0 + 3].set(1.0)
    return pq.astype(BF16), pk.astype(BF16), oneq, onek


def _head_expand():
    h = jnp.arange(SSM_HEADS)
    e = jnp.zeros((LANES, SSM_INNER), F32)
    cols = h[:, None] * SSM_HEAD_DIM + jnp.arange(SSM_HEAD_DIM)[None, :]
    return e.at[h[:, None], cols].set(1.0).astype(BF16)


def kernel(x, positions, norm_mix_g, w_in, b_gate, mla_q_norm_g, mla_w_uq, mla_kv_norm_g, mla_w_ukv,
           mla_q_gain, mla_k_gain, fox_q_gain, fox_k_gain, fox_b_f, ssm_conv_w, ssm_conv_b, ssm_dt_bias,
           ssm_A_log, ssm_D, ssm_norm_g, w_br_mla, w_br_fox, w_br_ssm, w_out, norm_ffn_g, ffn_w_up,
           ffn_conv_w, ffn_conv_b, ffn_w_down):
    B, S, D = x.shape
    T = B * S
    depth = w_in.shape[0]
    tile = min(256, S)
    ta = min(ATTN_TILE, S)

    pos = positions.reshape(T, 1).astype(F32)
    cos, sa, sb = _rope_tables(pos, tile)
    pq, pk, oneq, onek = _fox_placement()
    e01 = _head_expand()
    row = lambda v: v.reshape(1, -1).astype(F32)

    x2 = x.reshape(T, D).astype(F32)
    for l in range(depth):
        w_cat, w_small = _pack_w_in(w_in[l])
        u_mla, u_fox, u_z, u_xbc, u_gate, u_small = _inproj(x2, row(norm_mix_g[l]), w_cat, w_small, tile)

        wq = jnp.pad(mla_w_uq[l].reshape(MLA_Q_RANK, MLA_HEADS, MLA_QK_DIM),
                     ((0, 0), (0, 0), (0, HEAD_PAD - MLA_QK_DIM))).reshape(MLA_Q_RANK, -1).astype(BF16)
        wkv = mla_w_ukv[l].reshape(MLA_KV_RANK, MLA_HEADS, MLA_NOPE + MLA_V)
        wk = jnp.pad(wkv[:, :, :MLA_NOPE], ((0, 0), (0, 0), (0, HEAD_PAD - MLA_NOPE))
                     ).reshape(MLA_KV_RANK, -1).astype(BF16)
        wv = wkv[:, :, MLA_NOPE:].reshape(MLA_KV_RANK, -1).astype(BF16)
        q, k, vt, kmax = _mla_prep(u_mla, cos, sa, sb, row(mla_q_norm_g[l]), wq, row(mla_kv_norm_g[l]),
                                   wk, wv, _pad_lanes(mla_q_gain[l]), _pad_lanes(mla_k_gain[l]), B, S, ta)
        y_a = _attention(q, k, vt, kmax, ta, True, HEAD_PAD).reshape(T, -1)

        qf, kf, vtf, kmaxf = _fox_prep(u_fox, u_small, _pad_lanes(fox_b_f[l]), _pad_lanes(fox_q_gain[l]),
                                       _pad_lanes(fox_k_gain[l]), pq, pk, oneq, onek, B, S, ta)
        y_b = _attention(qf, kf, vtf, kmaxf, ta, False, FOX_HEAD_DIM).reshape(T, -1)

        dx = jnp.repeat(ssm_D[l].astype(F32), SSM_HEAD_DIM).reshape(1, -1)
        y_c = _ssm(u_z, u_xbc, u_small, ssm_conv_w[l].astype(F32), row(ssm_conv_b[l]),
                   _pad_lanes(ssm_dt_bias[l]), _pad_lanes(ssm_A_log[l]), dx, row(ssm_norm_g[l]), e01,
                   B, S, tile)

        x2 = _merge(x2, y_a, y_b, y_c, u_gate, row(b_gate[l]), w_br_mla[l].astype(BF16),
                    w_br_fox[l].astype(BF16), w_br_ssm[l].astype(BF16), w_out[l].astype(BF16), tile)
        x2 = _ffn(x2, row(norm_ffn_g[l]), ffn_w_up[l].astype(BF16), ffn_conv_w[l].astype(F32),
                  row(ffn_conv_b[l]), ffn_w_down[l].astype(BF16), S, tile)
    return x2.reshape(B, S, D).astype(x.dtype)
```

```python
import functools
import math

import jax
import jax.numpy as jnp
from jax import lax
from jax.experimental import pallas as pl
from jax.experimental.pallas import tpu as pltpu

F32 = jnp.float32
BF16 = jnp.bfloat16

D_MODEL = 1024
NORM_EPS = 1e-6
CHUNK = 64

MLA_HEADS = 8
MLA_Q_RANK = 384
MLA_KV_RANK = 256
MLA_NOPE = 64
MLA_ROPE = 32
MLA_QK_DIM = MLA_NOPE + MLA_ROPE
MLA_V = 64
ROPE_THETA = 10000.0

FOX_HEADS = 8
FOX_HEAD_DIM = 64
FOX_WIDTH = FOX_HEADS * FOX_HEAD_DIM

SSM_HEADS = 16
SSM_HEAD_DIM = 64
SSM_INNER = SSM_HEADS * SSM_HEAD_DIM
SSM_GROUPS = 2
SSM_STATE = 128
SSM_CONV = 4
SSM_XBC = SSM_INNER + 2 * SSM_GROUPS * SSM_STATE
SSM_GROUP_WIDTH = SSM_INNER // SSM_GROUPS

D_FF = 2816
FFN_CONV = 3

MLA_IN = MLA_Q_RANK + MLA_KV_RANK + MLA_ROPE
FOX_IN = 3 * FOX_WIDTH + FOX_HEADS
SSM_IN = SSM_INNER + SSM_XBC + SSM_HEADS

LANES = 128
SUBLANES = 8
HEAD_PAD = LANES
LOG2E = 1.4426950408889634
NEG_BIG = -1e30
VMEM_LIMIT = 52 * 1024 * 1024

MLA_SEG = MLA_Q_RANK + MLA_KV_RANK + LANES
SEG_MLA = (0, MLA_SEG)
SEG_FOX = (SEG_MLA[1], SEG_MLA[1] + 3 * FOX_WIDTH)
SEG_Z = (SEG_FOX[1], SEG_FOX[1] + SSM_INNER)
SEG_XBC = (SEG_Z[1], SEG_Z[1] + SSM_XBC)
SEG_GATE = (SEG_XBC[1], SEG_XBC[1] + 3 * D_MODEL)
W_CAT = SEG_GATE[1]
SMALL_W = 2 * LANES

AUG0 = FOX_HEAD_DIM


def _cparams(n_grid):
    return pltpu.CompilerParams(dimension_semantics=("arbitrary",) * n_grid,
                                vmem_limit_bytes=VMEM_LIMIT)


def _vmem_full():
    return pl.BlockSpec(memory_space=pltpu.VMEM)


def _dot(a, b):
    return jnp.dot(a, b, preferred_element_type=F32)


def _dot_nt(a, b):
    return lax.dot_general(a, b, (((1,), (1,)), ((), ())), preferred_element_type=F32)


def _split3(x):
    hi = x.astype(BF16)
    r = x - hi.astype(F32)
    mid = r.astype(BF16)
    lo = (r - mid.astype(F32)).astype(BF16)
    return hi, mid, lo


def _dot3_left(m01, x):
    hi, mid, lo = _split3(x)
    return _dot(m01, hi) + _dot(m01, mid) + _dot(m01, lo)


def _dot3_right(x, m01):
    hi, mid, lo = _split3(x)
    return _dot(hi, m01) + _dot(mid, m01) + _dot(lo, m01)


def _sigmoid(x):
    return 1.0 / (1.0 + jnp.exp(-x))


def _silu(x):
    return x * _sigmoid(x)


def _softplus(x):
    return jnp.maximum(x, 0.0) + jnp.log(1.0 + jnp.exp(-jnp.abs(x)))


def _tril_bf16(n):
    r = lax.broadcasted_iota(jnp.int32, (n, n), 0)
    c = lax.broadcasted_iota(jnp.int32, (n, n), 1)
    return jnp.where(r >= c, 1.0, 0.0).astype(BF16)


def _rope_kernel(pos_ref, inv_ref, cos_ref, sa_ref, sb_ref):
    ang = pos_ref[...] * inv_ref[...]
    lane = lax.broadcasted_iota(jnp.int32, ang.shape, 1)
    c = jnp.cos(ang)
    s = jnp.sin(ang)
    half = MLA_ROPE // 2
    first = (lane >= MLA_NOPE) & (lane < MLA_NOPE + half)
    second = (lane >= MLA_NOPE + half) & (lane < MLA_QK_DIM)
    cos_ref[...] = jnp.where(lane < MLA_NOPE, 1.0, jnp.where(first | second, c, 0.0))
    sa_ref[...] = jnp.where(first, -s, 0.0)
    sb_ref[...] = jnp.where(second, s, 0.0)


def _rope_tables(pos, ts):
    T = pos.shape[0]
    half = MLA_ROPE // 2
    inv = 1.0 / (ROPE_THETA ** (jnp.arange(0, MLA_ROPE, 2, dtype=F32) / MLA_ROPE))
    inv_lane = jnp.zeros((1, LANES), F32)
    inv_lane = inv_lane.at[0, MLA_NOPE:MLA_NOPE + half].set(inv)
    inv_lane = inv_lane.at[0, MLA_NOPE + half:MLA_QK_DIM].set(inv)
    tab = jax.ShapeDtypeStruct((T, LANES), F32)
    spec = pl.BlockSpec((ts, LANES), lambda i: (i, 0))
    return pl.pallas_call(
        _rope_kernel,
        grid=(T // ts,),
        in_specs=[pl.BlockSpec((ts, 1), lambda i: (i, 0)), pl.BlockSpec((1, LANES), lambda i: (0, 0))],
        out_specs=[spec, spec, spec],
        out_shape=[tab, tab, tab],
        compiler_params=_cparams(1),
        name="rope_tables",
    )(pos, inv_lane)


def _rope(x, cos, sa, sb):
    half = MLA_ROPE // 2
    return x * cos + pltpu.roll(x, LANES - half, axis=1) * sa + pltpu.roll(x, half, axis=1) * sb


IN_CH = 512


def _inproj_kernel(x_ref, g_ref, w_ref, wsm_ref, o_mla, o_fox, o_z, o_xbc, o_gate, o_small):
    x = x_ref[...]
    hn = (x * lax.rsqrt(jnp.mean(x * x, axis=-1, keepdims=True) + NORM_EPS) * g_ref[...]).astype(BF16)
    for o_ref, (c0, c1) in ((o_mla, SEG_MLA), (o_fox, SEG_FOX), (o_z, SEG_Z), (o_xbc, SEG_XBC),
                            (o_gate, SEG_GATE)):
        for cc in range(c0, c1, IN_CH):
            w = min(IN_CH, c1 - cc)
            o_ref[:, cc - c0:cc - c0 + w] = _dot(hn, w_ref[:, cc:cc + w]).astype(o_ref.dtype)
    o_small[...] = _dot(hn, wsm_ref[...])


def _inproj(x2, g, w_cat, w_small, tm):
    T = x2.shape[0]
    widths = [s[1] - s[0] for s in (SEG_MLA, SEG_FOX, SEG_Z, SEG_XBC, SEG_GATE)]
    out_shape = [jax.ShapeDtypeStruct((T, w), BF16) for w in widths]
    out_shape.append(jax.ShapeDtypeStruct((T, SMALL_W), F32))
    out_specs = [pl.BlockSpec((tm, w), lambda i: (i, 0)) for w in widths]
    out_specs.append(pl.BlockSpec((tm, SMALL_W), lambda i: (i, 0)))
    return pl.pallas_call(
        _inproj_kernel,
        grid=(T // tm,),
        in_specs=[pl.BlockSpec((tm, D_MODEL), lambda i: (i, 0)),
                  pl.BlockSpec((1, D_MODEL), lambda i: (0, 0)),
                  _vmem_full(), _vmem_full()],
        out_specs=out_specs,
        out_shape=out_shape,
        compiler_params=_cparams(1),
        name="inproj",
    )(x2, g, w_cat, w_small)


def _mla_prep_kernel(u_ref, cos_ref, sa_ref, sb_ref, gq_ref, wq_ref, gkv_ref, wk_ref, wv_ref,
                     qg_ref, kg_ref, qfix_ref, kfix_ref, q_out, k_out, vt_out):
    u = u_ref[...].astype(F32)
    cos = cos_ref[...]
    sa = sa_ref[...]
    sb = sb_ref[...]
    cq = u[:, :MLA_Q_RANK]
    ckv = u[:, MLA_Q_RANK:MLA_Q_RANK + MLA_KV_RANK]
    kr = u[:, MLA_Q_RANK + MLA_KV_RANK:]
    cqn = (cq * lax.rsqrt(jnp.mean(cq * cq, axis=-1, keepdims=True) + NORM_EPS) * gq_ref[...]).astype(BF16)
    ckvn = (ckv * lax.rsqrt(jnp.mean(ckv * ckv, axis=-1, keepdims=True) + NORM_EPS) * gkv_ref[...]).astype(BF16)
    q_all = _dot(cqn, wq_ref[...])
    k_all = _dot(ckvn, wk_ref[...])
    v_all = _dot(ckvn, wv_ref[...])
    kr_ss = jnp.sum(kr * kr, axis=-1, keepdims=True)
    kr_rot = _rope(kr * kg_ref[...], cos, sa, sb)
    q_scale = (MLA_QK_DIM ** -0.5) * LOG2E
    for h in range(MLA_HEADS):
        qh = q_all[:, h * HEAD_PAD:(h + 1) * HEAD_PAD]
        ms = jnp.sum(qh * qh, axis=-1, keepdims=True) * (1.0 / MLA_QK_DIM)
        qh = qh * lax.rsqrt(ms + NORM_EPS) * qg_ref[...]
        q_out[0, h] = (_rope(qh, cos, sa, sb) * q_scale + qfix_ref[...]).astype(q_out.dtype)
        kn = k_all[:, h * HEAD_PAD:(h + 1) * HEAD_PAD]
        ms = (jnp.sum(kn * kn, axis=-1, keepdims=True) + kr_ss) * (1.0 / MLA_QK_DIM)
        kh = (kn * kg_ref[...] + kr_rot) * lax.rsqrt(ms + NORM_EPS)
        k_out[0, h] = (kh + kfix_ref[...]).astype(k_out.dtype)
    vt_out[0, :, 0] = _vt_tile(v_all, MLA_HEADS)


def _mla_prep(u_mla, cos, sa, sb, gq, wq, gkv, wk, wv, qg, kg, qfix, kfix, B, S, ts):
    nb = S // ts
    H = MLA_HEADS
    tok = lambda b, i: (b * nb + i, 0)
    const = lambda b, i: (0, 0)
    lane_row = pl.BlockSpec((1, LANES), const)
    return pl.pallas_call(
        _mla_prep_kernel,
        grid=(B, nb),
        in_specs=[pl.BlockSpec((ts, MLA_SEG), tok),
                  pl.BlockSpec((ts, LANES), tok), pl.BlockSpec((ts, LANES), tok), pl.BlockSpec((ts, LANES), tok),
                  pl.BlockSpec((1, MLA_Q_RANK), const), _vmem_full(),
                  pl.BlockSpec((1, MLA_KV_RANK), const), _vmem_full(), _vmem_full(),
                  lane_row, lane_row, lane_row, lane_row],
        out_specs=[pl.BlockSpec((1, H, ts, HEAD_PAD), lambda b, i: (b, 0, i, 0)),
                   pl.BlockSpec((1, H, ts, HEAD_PAD), lambda b, i: (b, 0, i, 0)),
                   pl.BlockSpec((1, H, 1, V_ROWS, ts), lambda b, i: (b, 0, i, 0, 0))],
        out_shape=[jax.ShapeDtypeStruct((B, H, S, HEAD_PAD), BF16),
                   jax.ShapeDtypeStruct((B, H, S, HEAD_PAD), BF16),
                   jax.ShapeDtypeStruct((B, H, nb, V_ROWS, ts), BF16)],
        compiler_params=_cparams(2),
        name="mla_prep",
    )(u_mla, cos, sa, sb, gq, wq, gkv, wk, wv, qg, kg, qfix, kfix)


def _fox_prep_kernel(u_ref, f_ref, bf_ref, qg_ref, kg_ref, pq_ref, pk_ref, oneq_ref, onek_ref,
                     q_out, k_out, vt_out, carry):
    ts = u_ref.shape[0]

    @pl.when(pl.program_id(1) == 0)
    def _():
        carry[...] = jnp.zeros_like(carry)

    lane = lax.broadcasted_iota(jnp.int32, (ts, LANES), 1)
    f = f_ref[...] + bf_ref[...]
    log_f = jnp.minimum(f, 0.0) - jnp.log(1.0 + jnp.exp(-jnp.abs(f)))
    log_f = jnp.where(lane < FOX_HEADS, log_f * LOG2E, 0.0)
    cum = carry[...] + _dot3_left(_tril_bf16(ts), log_f)
    carry[...] = cum[ts - 1:ts, :]
    hi, mid, lo = _split3(cum)
    packed = (hi.astype(F32) + pltpu.roll(mid.astype(F32), FOX_HEADS, axis=1)
              + pltpu.roll(lo.astype(F32), 2 * FOX_HEADS, axis=1)).astype(BF16)
    aug_q = _dot(packed, pq_ref[...])
    aug_k = _dot(packed, pk_ref[...])

    u = u_ref[...].astype(F32)
    q_scale = (FOX_HEAD_DIM ** -0.5) * LOG2E
    head_lanes = lane < FOX_HEAD_DIM
    for h in range(FOX_HEADS):
        t0 = (h // 2) * LANES
        for src, gain, aug, ones, scale, out in (
                (0, qg_ref, aug_q, oneq_ref, q_scale, q_out),
                (FOX_WIDTH, kg_ref, aug_k, onek_ref, 1.0, k_out)):
            blk = u[:, src + t0:src + t0 + LANES]
            if h % 2 == 1:
                blk = pltpu.roll(blk, FOX_HEAD_DIM, axis=1)
            blk = jnp.where(head_lanes, blk, 0.0)
            ms = jnp.sum(blk * blk, axis=-1, keepdims=True) * (1.0 / FOX_HEAD_DIM)
            blk = blk * lax.rsqrt(ms + NORM_EPS) * (gain[...] * scale)
            blk = blk + aug[:, h * HEAD_PAD:(h + 1) * HEAD_PAD] + ones[...]
            out[0, h] = blk.astype(out.dtype)
    vt_out[0, :, 0] = _vt_tile(u[:, 2 * FOX_WIDTH:], FOX_HEADS)


def _fox_prep(u_fox, u_small, bf, qg, kg, pq, pk, oneq, onek, B, S, ts):
    nb = S // ts
    H = FOX_HEADS
    tok = lambda b, i: (b * nb + i, 0)
    const = lambda b, i: (0, 0)
    return pl.pallas_call(
        _fox_prep_kernel,
        grid=(B, nb),
        in_specs=[pl.BlockSpec((ts, 3 * FOX_WIDTH), tok),
                  pl.BlockSpec((ts, LANES), tok),
                  pl.BlockSpec((1, LANES), const), pl.BlockSpec((1, LANES), const),
                  pl.BlockSpec((1, LANES), const), _vmem_full(), _vmem_full(),
                  pl.BlockSpec((1, LANES), const), pl.BlockSpec((1, LANES), const)],
        out_specs=[pl.BlockSpec((1, H, ts, HEAD_PAD), lambda b, i: (b, 0, i, 0)),
                   pl.BlockSpec((1, H, ts, HEAD_PAD), lambda b, i: (b, 0, i, 0)),
                   pl.BlockSpec((1, H, 1, V_ROWS, ts), lambda b, i: (b, 0, i, 0, 0))],
        out_shape=[jax.ShapeDtypeStruct((B, H, S, HEAD_PAD), BF16),
                   jax.ShapeDtypeStruct((B, H, S, HEAD_PAD), BF16),
                   jax.ShapeDtypeStruct((B, H, nb, V_ROWS, ts), BF16)],
        scratch_shapes=[pltpu.VMEM((1, LANES), F32)],
        compiler_params=_cparams(2),
        name="fox_prep",
    )(u_fox, u_small, bf, qg, kg, pq, pk, oneq, onek)


ATTN_TILE = 512
V_ROWS = 64


def _vt_tile(v_all, heads):
    ts = v_all.shape[0]
    return v_all.T.reshape(heads, V_ROWS, ts).astype(BF16)


HEADS_PER_STEP = 4
QK_AHEAD = 1
BOUND_SLACK = 1.02
BOUND_STEP = 0.25
MAX_SAFE_BOUND = 60.0


def _score_bound(q_gain, k_gain, dim):
    bound = dim * jnp.max(jnp.abs(q_gain)) * jnp.max(jnp.abs(k_gain)) * (dim ** -0.5) * LOG2E * BOUND_SLACK
    bound = jnp.ceil(bound.astype(F32) / BOUND_STEP) * BOUND_STEP
    safe = bound < MAX_SAFE_BOUND
    return jnp.where(safe, bound, 0.0), safe.astype(jnp.int32).reshape(1)


def _attn_kernel(flag_ref, q_ref, k_ref, vt_ref, o_ref, *, tq, chunk_causal):
    i = pl.program_id(2)
    hs = q_ref.shape[1]
    dv = vt_ref.shape[3]

    def scores(item):
        j, hh = item
        kj = k_ref[0, hh, pl.ds(pl.multiple_of(j * tq, tq), tq), :]
        return _dot_nt(kj, q_ref[0, hh])

    def diag_mask():
        r = lax.broadcasted_iota(jnp.int32, (tq, tq), 0)
        c = lax.broadcasted_iota(jnp.int32, (tq, tq), 1)
        shift = CHUNK.bit_length() - 1
        return ((r >> shift) <= (c >> shift)) if chunk_causal else (r <= c)

    def sweep(update, init, finish):
        def tiles(js, state, diagonal):
            ok = diag_mask() if diagonal else None
            items = [(j, hh) for j in js for hh in range(hs)]
            state = list(state)
            pending = [scores(item) for item in items[:QK_AHEAD]]
            for n, (j, hh) in enumerate(items):
                s = pending.pop(0)
                if n + QK_AHEAD < len(items):
                    pending.append(scores(items[n + QK_AHEAD]))
                if diagonal:
                    s = jnp.where(ok, s, NEG_BIG)
                state[hh] = update(state[hh], s, vt_ref[0, hh, j])
            return tuple(state)

        pairs = lax.shift_right_logical(i, 1)
        state = lax.fori_loop(0, pairs, lambda jj, st: tiles([2 * jj, 2 * jj + 1], st, False), init)
        state = lax.fori_loop(0, i & 1, lambda _, st: tiles([i - 1], st, False), state)
        state = tiles([i], state, True)
        o = jnp.concatenate([finish(st) for st in state], axis=0)
        o_ref[0] = o.T.astype(o_ref.dtype)

    def column_sums(p):
        return p.reshape(tq // SUBLANES, SUBLANES, tq).sum(axis=0)

    @pl.when(flag_ref[0] == 1)
    def _():
        def update(st, s, vt):
            l8, acc = st
            p = jnp.exp2(s)
            return l8 + column_sums(p), acc + _dot(vt, p.astype(BF16))

        init = tuple((jnp.zeros((SUBLANES, tq), F32), jnp.zeros((dv, tq), F32)) for _ in range(hs))
        sweep(update, init, lambda st: st[1] * (1.0 / jnp.sum(st[0], axis=0, keepdims=True)))

    @pl.when(flag_ref[0] != 1)
    def _():
        def update(st, s, vt):
            m, l8, acc = st
            m_new = jnp.maximum(m, jnp.max(s, axis=0, keepdims=True))
            alpha = jnp.exp2(m - m_new)
            p = jnp.exp2(s - m_new)
            return m_new, alpha * l8 + column_sums(p), alpha * acc + _dot(vt, p.astype(BF16))

        init = tuple((jnp.full((1, tq), NEG_BIG, F32), jnp.zeros((SUBLANES, tq), F32),
                      jnp.zeros((dv, tq), F32)) for _ in range(hs))
        sweep(update, init, lambda st: st[2] * (1.0 / jnp.sum(st[1], axis=0, keepdims=True)))


def _attention(flag, q, k, vt, tq, chunk_causal):
    B, H, S, _ = q.shape
    dv = vt.shape[3]
    hs = HEADS_PER_STEP
    assert vt.shape[4] == tq and (hs * dv) % LANES == 0 and H % hs == 0
    nq = S // tq
    return pl.pallas_call(
        functools.partial(_attn_kernel, tq=tq, chunk_causal=chunk_causal),
        grid=(B, H // hs, nq),
        in_specs=[pl.BlockSpec(memory_space=pltpu.SMEM),
                  pl.BlockSpec((1, hs, tq, HEAD_PAD), lambda b, h, i: (b, h, i, 0)),
                  pl.BlockSpec((1, hs, S, HEAD_PAD), lambda b, h, i: (b, h, 0, 0)),
                  pl.BlockSpec((1, hs, nq, dv, tq), lambda b, h, i: (b, h, 0, 0, 0))],
        out_specs=pl.BlockSpec((1, tq, hs * dv), lambda b, h, i: (b, i, h)),
        out_shape=jax.ShapeDtypeStruct((B, S, H * dv), BF16),
        compiler_params=_cparams(3),
        name="attn_mla" if chunk_causal else "attn_fox",
    )(flag, q, k, vt)


def _ssm_kernel(z_ref, xbc_ref, dt_ref, cw_ref, cb_ref, dtb_ref, alog_ref, dx_ref, ng_ref, e_ref,
                o_ref, xpad, hst, *, lc):
    pad = SUBLANES

    @pl.when(pl.program_id(1) == 0)
    def _():
        xpad[0:pad, :] = jnp.zeros((pad, SSM_XBC), F32)
        hst[...] = jnp.zeros_like(hst)

    xpad[pad:pad + lc, :] = xbc_ref[...].astype(F32)
    xp = xpad[...]
    conv = cb_ref[...] + cw_ref[SSM_CONV - 1:SSM_CONV, :] * xp[pad:, :]
    for back in range(1, SSM_CONV):
        kk = SSM_CONV - 1 - back
        conv = conv + cw_ref[kk:kk + 1, :] * pltpu.roll(xp, back, axis=0)[pad:, :]
    xpad[0:pad, :] = xpad[lc:lc + pad, :]
    xc = _silu(conv)
    xs = xc[:, :SSM_INNER]
    bm = xc[:, SSM_INNER:SSM_INNER + SSM_GROUPS * SSM_STATE]
    cm = xc[:, SSM_INNER + SSM_GROUPS * SSM_STATE:]

    lane = lax.broadcasted_iota(jnp.int32, (lc, LANES), 1)
    dt = jnp.where(lane < SSM_HEADS, _softplus(dt_ref[...] + dtb_ref[...]), 0.0)
    a_neg = -jnp.exp(alog_ref[...]) * LOG2E
    a_cum = _dot3_left(_tril_bf16(lc), dt * a_neg)
    head_lanes = lane < SSM_HEADS
    ea = jnp.where(head_lanes, jnp.exp2(a_cum), 0.0)
    dte = jnp.where(head_lanes, jnp.exp2(a_cum[lc - 1:lc, :] - a_cum), 0.0)
    e01 = e_ref[...]

    def expand(v):
        hi, mid, lo = _split3(v)
        packed = (hi.astype(F32) + pltpu.roll(mid.astype(F32), SSM_HEADS, axis=1)
                  + pltpu.roll(lo.astype(F32), 2 * SSM_HEADS, axis=1)).astype(BF16)
        return _dot(packed, e01)

    dt_x = expand(dt)
    ea_x = expand(ea)
    dte_x = expand(dte)
    xdt = xs * dt_x
    a_t = a_cum.T

    row = lax.broadcasted_iota(jnp.int32, (lc, lc), 0)
    col = lax.broadcasted_iota(jnp.int32, (lc, lc), 1)
    causal = row >= col
    lane_first = lane < SSM_HEAD_DIM
    heads_per_group = SSM_HEADS // SSM_GROUPS
    y_blocks = []
    for g in range(SSM_GROUPS):
        bg = bm[:, g * SSM_STATE:(g + 1) * SSM_STATE]
        cg = cm[:, g * SSM_STATE:(g + 1) * SSM_STATE].astype(BF16)
        cb = _dot_nt(cg, bg.astype(BF16))
        gs = slice(g * SSM_GROUP_WIDTH, (g + 1) * SSM_GROUP_WIDTH)
        h_prev = hst[g]
        y_off = _dot(cg, h_prev.astype(BF16)) * ea_x[:, gs]
        xw = (xdt[:, gs] * dte_x[:, gs]).astype(BF16)
        s_new = _dot(bg.T.astype(BF16), xw)
        hst[g] = h_prev * ea_x[lc - 1:lc, gs] + s_new
        for hp in range(heads_per_group // 2):
            t0 = g * SSM_GROUP_WIDTH + hp * LANES
            x_pair = xdt[:, t0:t0 + LANES]
            acc = None
            for e in range(2):
                h = g * heads_per_group + hp * 2 + e
                seg = a_cum[:, h:h + 1] - a_t[h:h + 1, :]
                decay = jnp.exp2(jnp.where(causal, seg, NEG_BIG))
                gmat = (cb * decay).astype(BF16)
                xh = jnp.where(lane_first if e == 0 else ~lane_first, x_pair, 0.0).astype(BF16)
                part = _dot(gmat, xh)
                acc = part if acc is None else acc + part
            y_blocks.append(acc + y_off[:, hp * LANES:(hp + 1) * LANES])
    y = jnp.concatenate(y_blocks, axis=1) + xs * dx_ref[...]
    zf = z_ref[...].astype(F32)
    yg = y * _silu(zf)
    o_ref[...] = (yg * lax.rsqrt(jnp.mean(yg * yg, axis=-1, keepdims=True) + NORM_EPS)
                  * ng_ref[...]).astype(o_ref.dtype)


def _ssm(u_z, u_xbc, u_small, cw, cb, dtb, alog, dx, ng, e01, B, S, lc):
    nb = S // lc
    tok = lambda b, i: (b * nb + i, 0)
    const = lambda b, i: (0, 0)
    return pl.pallas_call(
        functools.partial(_ssm_kernel, lc=lc),
        grid=(B, nb),
        in_specs=[pl.BlockSpec((lc, SSM_INNER), tok),
                  pl.BlockSpec((lc, SSM_XBC), tok),
                  pl.BlockSpec((lc, LANES), lambda b, i: (b * nb + i, 1)),
                  pl.BlockSpec((SSM_CONV, SSM_XBC), const), pl.BlockSpec((1, SSM_XBC), const),
                  pl.BlockSpec((1, LANES), const), pl.BlockSpec((1, LANES), const),
                  pl.BlockSpec((1, SSM_INNER), const), pl.BlockSpec((1, SSM_INNER), const),
                  _vmem_full()],
        out_specs=pl.BlockSpec((lc, SSM_INNER), tok),
        out_shape=jax.ShapeDtypeStruct((B * S, SSM_INNER), BF16),
        scratch_shapes=[pltpu.VMEM((lc + SUBLANES, SSM_XBC), F32),
                        pltpu.VMEM((SSM_GROUPS, SSM_STATE, SSM_GROUP_WIDTH), F32)],
        compiler_params=_cparams(2),
        name="ssm",
    )(u_z, u_xbc, u_small, cw, cb, dtb, alog, dx, ng, e01)


def _merge_kernel(x_ref, ya_ref, yb_ref, yc_ref, ug_ref, bg_ref, wa_ref, wb_ref, wc_ref, wo_ref, o_ref):
    ys = (_dot(ya_ref[...], wa_ref[...]), _dot(yb_ref[...], wb_ref[...]), _dot(yc_ref[...], wc_ref[...]))
    merged = None
    for n, y in enumerate(ys):
        cs = slice(n * D_MODEL, (n + 1) * D_MODEL)
        gate = _sigmoid(ug_ref[:, cs].astype(F32) + bg_ref[:, cs])
        merged = gate * y if merged is None else merged + gate * y
    o_ref[...] = x_ref[...] + _dot(merged.astype(BF16), wo_ref[...])


def _merge(x2, ya, yb, yc, ug, bg, wa, wb, wc, wo, tm):
    T = x2.shape[0]
    row = lambda w: pl.BlockSpec((tm, w), lambda i: (i, 0))
    return pl.pallas_call(
        _merge_kernel,
        grid=(T // tm,),
        in_specs=[row(D_MODEL), row(MLA_HEADS * MLA_V), row(FOX_WIDTH), row(SSM_INNER), row(3 * D_MODEL),
                  pl.BlockSpec((1, 3 * D_MODEL), lambda i: (0, 0)),
                  _vmem_full(), _vmem_full(), _vmem_full(), _vmem_full()],
        out_specs=row(D_MODEL),
        out_shape=jax.ShapeDtypeStruct((T, D_MODEL), F32),
        compiler_params=_cparams(1),
        name="merge",
    )(x2, ya, yb, yc, ug, bg, wa, wb, wc, wo)


FF_CH = 256


def _ffn_kernel(x_ref, g_ref, wup_ref, cw_ref, cb_ref, wdn_ref, o_ref, abuf, hbuf, *, tm, blocks_per_seq):
    pad = SUBLANES

    @pl.when(pl.program_id(0) % blocks_per_seq == 0)
    def _():
        abuf[0:pad, :] = jnp.zeros((pad, 2 * D_FF), F32)

    x = x_ref[...]
    hn = (x * lax.rsqrt(jnp.mean(x * x, axis=-1, keepdims=True) + NORM_EPS) * g_ref[...]).astype(BF16)
    for c0 in range(0, D_FF, FF_CH):
        halves = []
        for base in (c0, D_FF + c0):
            cs = slice(base, base + FF_CH)
            abuf[pad:pad + tm, cs] = _dot(hn, wup_ref[:, cs])
            ap = abuf[:, cs]
            conv = cb_ref[:, cs] + cw_ref[FFN_CONV - 1:FFN_CONV, cs] * ap[pad:, :]
            for back in range(1, FFN_CONV):
                kk = FFN_CONV - 1 - back
                conv = conv + cw_ref[kk:kk + 1, cs] * pltpu.roll(ap, back, axis=0)[pad:, :]
            abuf[0:pad, cs] = abuf[tm:tm + pad, cs]
            halves.append(conv)
        hbuf[:, c0:c0 + FF_CH] = (_silu(halves[0]) * halves[1]).astype(BF16)
    o_ref[...] = x + _dot(hbuf[...], wdn_ref[...])


def _ffn(x2, g, wup, cw, cb, wdn, S, tm):
    T = x2.shape[0]
    return pl.pallas_call(
        functools.partial(_ffn_kernel, tm=tm, blocks_per_seq=S // tm),
        grid=(T // tm,),
        in_specs=[pl.BlockSpec((tm, D_MODEL), lambda i: (i, 0)),
                  pl.BlockSpec((1, D_MODEL), lambda i: (0, 0)),
                  _vmem_full(),
                  pl.BlockSpec((FFN_CONV, 2 * D_FF), lambda i: (0, 0)),
                  pl.BlockSpec((1, 2 * D_FF), lambda i: (0, 0)),
                  _vmem_full()],
        out_specs=pl.BlockSpec((tm, D_MODEL), lambda i: (i, 0)),
        out_shape=jax.ShapeDtypeStruct((T, D_MODEL), F32),
        scratch_shapes=[pltpu.VMEM((tm + SUBLANES, 2 * D_FF), F32),
                        pltpu.VMEM((tm, D_FF), BF16)],
        compiler_params=_cparams(1),
        name="ffn",
    )(x2, g, wup, cw, cb, wdn)


def _pad_lanes(v, width=LANES, offset=0):
    out = jnp.zeros((1, width), F32)
    return out.at[0, offset:offset + v.shape[0]].set(v.astype(F32))


def _pack_w_in(w):
    zeros = lambda n: jnp.zeros((D_MODEL, n), w.dtype)
    o_fox = MLA_IN
    o_ssm = MLA_IN + FOX_IN
    o_gate = o_ssm + SSM_IN
    kr0 = MLA_Q_RANK + MLA_KV_RANK
    w_cat = jnp.concatenate([
        w[:, :kr0], zeros(MLA_NOPE), w[:, kr0:MLA_IN], zeros(LANES - MLA_QK_DIM),
        w[:, o_fox:o_fox + 3 * FOX_WIDTH],
        w[:, o_ssm:o_ssm + SSM_INNER + SSM_XBC],
        w[:, o_gate:],
    ], axis=1).astype(BF16)
    assert w_cat.shape[1] == W_CAT
    f0 = o_fox + 3 * FOX_WIDTH
    d0 = o_ssm + SSM_INNER + SSM_XBC
    w_small = jnp.concatenate([
        w[:, f0:f0 + FOX_HEADS], zeros(LANES - FOX_HEADS),
        w[:, d0:d0 + SSM_HEADS], zeros(LANES - SSM_HEADS),
    ], axis=1).astype(BF16)
    return w_cat, w_small


def _fox_placement():
    rows = jnp.arange(3 * FOX_HEADS)
    piece = rows // FOX_HEADS
    head = rows % FOX_HEADS
    pq = jnp.zeros((LANES, FOX_HEADS * HEAD_PAD), F32).at[rows, head * HEAD_PAD + AUG0 + piece].set(1.0)
    pk = jnp.zeros((LANES, FOX_HEADS * HEAD_PAD), F32).at[rows, head * HEAD_PAD + AUG0 + 3 + piece].set(-1.0)
    oneq = jnp.zeros((1, LANES), F32).at[0, AUG0 + 3:AUG0 + 6].set(1.0)
    onek = jnp.zeros((1, LANES), F32).at[0, AUG0:AUG0 + 3].set(1.0)
    return pq.astype(BF16), pk.astype(BF16), oneq, onek


def _shift_rows(bound, lane):
    qfix = jnp.zeros((1, LANES), F32).at[0, lane].set(-bound)
    kfix = jnp.zeros((1, LANES), F32).at[0, lane].set(1.0)
    return qfix, kfix


def _head_expand():
    h = jnp.arange(3 * SSM_HEADS)
    e = jnp.zeros((LANES, SSM_INNER), F32)
    cols = (h % SSM_HEADS)[:, None] * SSM_HEAD_DIM + jnp.arange(SSM_HEAD_DIM)[None, :]
    return e.at[h[:, None], cols].set(1.0).astype(BF16)


def kernel(x, positions, norm_mix_g, w_in, b_gate, mla_q_norm_g, mla_w_uq, mla_kv_norm_g, mla_w_ukv,
           mla_q_gain, mla_k_gain, fox_q_gain, fox_k_gain, fox_b_f, ssm_conv_w, ssm_conv_b, ssm_dt_bias,
           ssm_A_log, ssm_D, ssm_norm_g, w_br_mla, w_br_fox, w_br_ssm, w_out, norm_ffn_g, ffn_w_up,
           ffn_conv_w, ffn_conv_b, ffn_w_down):
    B, S, D = x.shape
    T = B * S
    depth = w_in.shape[0]
    tile = min(256, S)
    ta = min(ATTN_TILE, S)

    pos = positions.reshape(T, 1).astype(F32)
    cos, sa, sb = _rope_tables(pos, tile)
    pq, pk, oneq, onek = _fox_placement()
    e01 = _head_expand()
    row = lambda v: v.reshape(1, -1).astype(F32)

    x2 = x.reshape(T, D).astype(F32)
    for l in range(depth):
        w_cat, w_small = _pack_w_in(w_in[l])
        u_mla, u_fox, u_z, u_xbc, u_gate, u_small = _inproj(x2, row(norm_mix_g[l]), w_cat, w_small, tile)

        wq = jnp.pad(mla_w_uq[l].reshape(MLA_Q_RANK, MLA_HEADS, MLA_QK_DIM),
                     ((0, 0), (0, 0), (0, HEAD_PAD - MLA_QK_DIM))).reshape(MLA_Q_RANK, -1).astype(BF16)
        wkv = mla_w_ukv[l].reshape(MLA_KV_RANK, MLA_HEADS, MLA_NOPE + MLA_V)
        wk = jnp.pad(wkv[:, :, :MLA_NOPE], ((0, 0), (0, 0), (0, HEAD_PAD - MLA_NOPE))
                     ).reshape(MLA_KV_RANK, -1).astype(BF16)
        wv = wkv[:, :, MLA_NOPE:].reshape(MLA_KV_RANK, -1).astype(BF16)
        bound, flag = _score_bound(mla_q_gain[l], mla_k_gain[l], MLA_QK_DIM)
        qfix, kfix = _shift_rows(bound, MLA_QK_DIM)
        q, k, vt = _mla_prep(u_mla, cos, sa, sb, row(mla_q_norm_g[l]), wq, row(mla_kv_norm_g[l]), wk, wv,
                             _pad_lanes(mla_q_gain[l]), _pad_lanes(mla_k_gain[l]), qfix, kfix, B, S, ta)
        y_a = _attention(flag, q, k, vt, ta, True).reshape(T, -1)

        bound, flag = _score_bound(fox_q_gain[l], fox_k_gain[l], FOX_HEAD_DIM)
        qfix, kfix = _shift_rows(bound, AUG0 + 6)
        qf, kf, vtf = _fox_prep(u_fox, u_small, _pad_lanes(fox_b_f[l]), _pad_lanes(fox_q_gain[l]),
                                _pad_lanes(fox_k_gain[l]), pq, pk, oneq + qfix, onek + kfix, B, S, ta)
        y_b = _attention(flag, qf, kf, vtf, ta, False).reshape(T, -1)

        dx = jnp.repeat(ssm_D[l].astype(F32), SSM_HEAD_DIM).reshape(1, -1)
        y_c = _ssm(u_z, u_xbc, u_small, ssm_conv_w[l].astype(F32), row(ssm_conv_b[l]),
                   _pad_lanes(ssm_dt_bias[l]), _pad_lanes(ssm_A_log[l]), dx, row(ssm_norm_g[l]), e01,
                   B, S, tile)

        x2 = _merge(x2, y_a, y_b, y_c, u_gate, row(b_gate[l]), w_br_mla[l].astype(BF16),
                    w_br_fox[l].astype(BF16), w_br_ssm[l].astype(BF16), w_out[l].astype(BF16), tile)
        x2 = _ffn(x2, row(norm_ffn_g[l]), ffn_w_up[l].astype(BF16), ffn_conv_w[l].astype(F32),
                  row(ffn_conv_b[l]), ffn_w_down[l].astype(BF16), S, tile)
    return x2.reshape(B, S, D).astype(x.dtype)
```

```python
import functools

import numpy as np
import jax
import jax.numpy as jnp
from jax import lax
from jax.experimental import pallas as pl
from jax.experimental.pallas import tpu as pltpu

F32 = jnp.float32
BF16 = jnp.bfloat16

D_MODEL = 1024
NORM_EPS = 1e-6
CHUNK = 64

MLA_HEADS = 8
MLA_Q_RANK = 384
MLA_KV_RANK = 256
MLA_NOPE = 64
MLA_ROPE = 32
MLA_QK_DIM = MLA_NOPE + MLA_ROPE
MLA_V = 64
ROPE_THETA = 10000.0

FOX_HEADS = 8
FOX_HEAD_DIM = 64
FOX_WIDTH = FOX_HEADS * FOX_HEAD_DIM

SSM_HEADS = 16
SSM_HEAD_DIM = 64
SSM_INNER = SSM_HEADS * SSM_HEAD_DIM
SSM_GROUPS = 2
SSM_STATE = 128
SSM_CONV = 4
SSM_XBC = SSM_INNER + 2 * SSM_GROUPS * SSM_STATE
SSM_GROUP_WIDTH = SSM_INNER // SSM_GROUPS

D_FF = 2816
FFN_CONV = 3

MLA_IN = MLA_Q_RANK + MLA_KV_RANK + MLA_ROPE
FOX_IN = 3 * FOX_WIDTH + FOX_HEADS
SSM_IN = SSM_INNER + SSM_XBC + SSM_HEADS

LANES = 128
SUBLANES = 8
HEAD_PAD = LANES
LOG2E = 1.4426950408889634
NEG_BIG = -1e30
VMEM_LIMIT = 52 * 1024 * 1024

MLA_SEG = MLA_Q_RANK + MLA_KV_RANK + LANES
SMALL_W = 2 * LANES

AUG0 = FOX_HEAD_DIM


def _cparams(n_grid):
    return pltpu.CompilerParams(dimension_semantics=("arbitrary",) * n_grid,
                                vmem_limit_bytes=VMEM_LIMIT)


def _vmem_full():
    return pl.BlockSpec(memory_space=pltpu.VMEM)


def _dot(a, b):
    return jnp.dot(a, b, preferred_element_type=F32)


def _dot_nt(a, b):
    return lax.dot_general(a, b, (((1,), (1,)), ((), ())), preferred_element_type=F32)


def _split3(x):
    hi = x.astype(BF16)
    r = x - hi.astype(F32)
    mid = r.astype(BF16)
    lo = (r - mid.astype(F32)).astype(BF16)
    return hi, mid, lo


def _dot3_left(m01, x):
    hi, mid, lo = _split3(x)
    return _dot(m01, hi) + _dot(m01, mid) + _dot(m01, lo)


def _dot3_right(x, m01):
    hi, mid, lo = _split3(x)
    return _dot(hi, m01) + _dot(mid, m01) + _dot(lo, m01)


def _sigmoid(x):
    return 1.0 / (1.0 + jnp.exp(-x))


def _silu(x):
    return x * _sigmoid(x)


def _softplus(x):
    return jnp.maximum(x, 0.0) + jnp.log(1.0 + jnp.exp(-jnp.abs(x)))


def _tril_bf16(n):
    r = lax.broadcasted_iota(jnp.int32, (n, n), 0)
    c = lax.broadcasted_iota(jnp.int32, (n, n), 1)
    return jnp.where(r >= c, 1.0, 0.0).astype(BF16)


def _rope_kernel(pos_ref, inv_ref, cos_ref, sa_ref, sb_ref):
    ang = pos_ref[...] * inv_ref[...]
    lane = lax.broadcasted_iota(jnp.int32, ang.shape, 1)
    c = jnp.cos(ang)
    s = jnp.sin(ang)
    half = MLA_ROPE // 2
    first = (lane >= MLA_NOPE) & (lane < MLA_NOPE + half)
    second = (lane >= MLA_NOPE + half) & (lane < MLA_QK_DIM)
    cos_ref[...] = jnp.where(lane < MLA_NOPE, 1.0, jnp.where(first | second, c, 0.0))
    sa_ref[...] = jnp.where(first, -s, 0.0)
    sb_ref[...] = jnp.where(second, s, 0.0)


def _rope_tables(pos, ts):
    T = pos.shape[0]
    half = MLA_ROPE // 2
    inv = 1.0 / (ROPE_THETA ** (jnp.arange(0, MLA_ROPE, 2, dtype=F32) / MLA_ROPE))
    inv_lane = jnp.concatenate([jnp.zeros((MLA_NOPE,), F32), inv, inv,
                                jnp.zeros((LANES - MLA_QK_DIM,), F32)]).reshape(1, LANES)
    tab = jax.ShapeDtypeStruct((T, LANES), F32)
    spec = pl.BlockSpec((ts, LANES), lambda i: (i, 0))
    return pl.pallas_call(
        _rope_kernel,
        grid=(T // ts,),
        in_specs=[pl.BlockSpec((ts, 1), lambda i: (i, 0)), pl.BlockSpec((1, LANES), lambda i: (0, 0))],
        out_specs=[spec, spec, spec],
        out_shape=[tab, tab, tab],
        compiler_params=_cparams(1),
        name="rope_tables",
    )(pos, inv_lane)


def _rope(x, cos, sa, sb):
    half = MLA_ROPE // 2
    return x * cos + pltpu.roll(x, LANES - half, axis=1) * sa + pltpu.roll(x, half, axis=1) * sb


IN_CH = 512
DENSE_TILE = 512


N_SEG = 6


def _inproj_kernel(x_ref, g_ref, *refs):
    w_refs, o_refs = refs[:N_SEG], refs[N_SEG:]
    x = x_ref[...]
    hn = (x * lax.rsqrt(jnp.mean(x * x, axis=-1, keepdims=True) + NORM_EPS) * g_ref[...]).astype(BF16)
    for w_ref, o_ref in zip(w_refs, o_refs):
        n = w_ref.shape[1]
        for cc in range(0, n, IN_CH):
            w = min(IN_CH, n - cc)
            o_ref[:, cc:cc + w] = _dot(hn, w_ref[:, cc:cc + w]).astype(o_ref.dtype)


def _inproj(x2, g, weights, tm):
    T = x2.shape[0]
    widths = [w.shape[1] for w in weights]
    dtypes = [BF16] * (N_SEG - 1) + [F32]
    return pl.pallas_call(
        _inproj_kernel,
        grid=(T // tm,),
        in_specs=[pl.BlockSpec((tm, D_MODEL), lambda i: (i, 0)),
                  pl.BlockSpec((1, D_MODEL), lambda i: (0, 0))] + [_vmem_full()] * N_SEG,
        out_specs=[pl.BlockSpec((tm, w), lambda i: (i, 0)) for w in widths],
        out_shape=[jax.ShapeDtypeStruct((T, w), d) for w, d in zip(widths, dtypes)],
        compiler_params=_cparams(1),
        name="inproj",
    )(x2, g, *weights)


def _mla_prep_kernel(u_ref, cos_ref, sa_ref, sb_ref, gq_ref, wq_ref, gkv_ref, wk_ref, wv_ref,
                     qg_ref, kg_ref, qfix_ref, kfix_ref, q_out, k_out, vt_out):
    u = u_ref[...].astype(F32)
    cos = cos_ref[...]
    sa = sa_ref[...]
    sb = sb_ref[...]
    cq = u[:, :MLA_Q_RANK]
    ckv = u[:, MLA_Q_RANK:MLA_Q_RANK + MLA_KV_RANK]
    kr = u[:, MLA_Q_RANK + MLA_KV_RANK:]
    cqn = (cq * lax.rsqrt(jnp.mean(cq * cq, axis=-1, keepdims=True) + NORM_EPS) * gq_ref[...]).astype(BF16)
    ckvn = (ckv * lax.rsqrt(jnp.mean(ckv * ckv, axis=-1, keepdims=True) + NORM_EPS) * gkv_ref[...]).astype(BF16)
    q_all = _dot(cqn, wq_ref[...])
    k_all = _dot(ckvn, wk_ref[...])
    v_all = _dot(ckvn, wv_ref[...])
    kr_ss = jnp.sum(kr * kr, axis=-1, keepdims=True)
    kr_rot = _rope(kr * kg_ref[...], cos, sa, sb)
    q_scale = (MLA_QK_DIM ** -0.5) * LOG2E
    for h in range(MLA_HEADS):
        qh = q_all[:, h * HEAD_PAD:(h + 1) * HEAD_PAD]
        ms = jnp.sum(qh * qh, axis=-1, keepdims=True) * (1.0 / MLA_QK_DIM)
        qh = qh * lax.rsqrt(ms + NORM_EPS) * qg_ref[...]
        q_out[0, h] = (_rope(qh, cos, sa, sb) * q_scale + qfix_ref[...]).astype(q_out.dtype)
        kn = k_all[:, h * HEAD_PAD:(h + 1) * HEAD_PAD]
        ms = (jnp.sum(kn * kn, axis=-1, keepdims=True) + kr_ss) * (1.0 / MLA_QK_DIM)
        kh = (kn * kg_ref[...] + kr_rot) * lax.rsqrt(ms + NORM_EPS)
        k_out[0, h] = (kh + kfix_ref[...]).astype(k_out.dtype)
    vt_out[0, :, 0] = _vt_tile(v_all, MLA_HEADS)


def _mla_prep(u_mla, cos, sa, sb, gq, wq, gkv, wk, wv, qg, kg, qfix, kfix, B, S, ts):
    nb = S // ts
    H = MLA_HEADS
    tok = lambda b, i: (b * nb + i, 0)
    const = lambda b, i: (0, 0)
    lane_row = pl.BlockSpec((1, LANES), const)
    return pl.pallas_call(
        _mla_prep_kernel,
        grid=(B, nb),
        in_specs=[pl.BlockSpec((ts, MLA_SEG), tok),
                  pl.BlockSpec((ts, LANES), tok), pl.BlockSpec((ts, LANES), tok), pl.BlockSpec((ts, LANES), tok),
                  pl.BlockSpec((1, MLA_Q_RANK), const), _vmem_full(),
                  pl.BlockSpec((1, MLA_KV_RANK), const), _vmem_full(), _vmem_full(),
                  lane_row, lane_row, lane_row, lane_row],
        out_specs=[pl.BlockSpec((1, H, ts, HEAD_PAD), lambda b, i: (b, 0, i, 0)),
                   pl.BlockSpec((1, H, ts, HEAD_PAD), lambda b, i: (b, 0, i, 0)),
                   pl.BlockSpec((1, H, 1, V_ROWS, ts), lambda b, i: (b, 0, i, 0, 0))],
        out_shape=[jax.ShapeDtypeStruct((B, H, S, HEAD_PAD), BF16),
                   jax.ShapeDtypeStruct((B, H, S, HEAD_PAD), BF16),
                   jax.ShapeDtypeStruct((B, H, nb, V_ROWS, ts), BF16)],
        compiler_params=_cparams(2),
        name="mla_prep",
    )(u_mla, cos, sa, sb, gq, wq, gkv, wk, wv, qg, kg, qfix, kfix)


def _fox_prep_kernel(u_ref, f_ref, bf_ref, qg_ref, kg_ref, pq_ref, pk_ref, oneq_ref, onek_ref,
                     q_out, k_out, vt_out, carry):
    ts = u_ref.shape[0]

    @pl.when(pl.program_id(1) == 0)
    def _():
        carry[...] = jnp.zeros_like(carry)

    lane = lax.broadcasted_iota(jnp.int32, (ts, LANES), 1)
    f = f_ref[...] + bf_ref[...]
    log_f = jnp.minimum(f, 0.0) - jnp.log(1.0 + jnp.exp(-jnp.abs(f)))
    log_f = jnp.where(lane < FOX_HEADS, log_f * LOG2E, 0.0)
    cum = carry[...] + _dot3_left(_tril_bf16(ts), log_f)
    carry[...] = cum[ts - 1:ts, :]
    hi, mid, lo = _split3(cum)
    packed = (hi.astype(F32) + pltpu.roll(mid.astype(F32), FOX_HEADS, axis=1)
              + pltpu.roll(lo.astype(F32), 2 * FOX_HEADS, axis=1)).astype(BF16)
    aug_q = _dot(packed, pq_ref[...])
    aug_k = _dot(packed, pk_ref[...])

    u = u_ref[...].astype(F32)
    q_scale = (FOX_HEAD_DIM ** -0.5) * LOG2E
    head_lanes = lane < FOX_HEAD_DIM
    for h in range(FOX_HEADS):
        t0 = (h // 2) * LANES
        for src, gain, aug, ones, scale, out in (
                (0, qg_ref, aug_q, oneq_ref, q_scale, q_out),
                (FOX_WIDTH, kg_ref, aug_k, onek_ref, 1.0, k_out)):
            blk = u[:, src + t0:src + t0 + LANES]
            if h % 2 == 1:
                blk = pltpu.roll(blk, FOX_HEAD_DIM, axis=1)
            blk = jnp.where(head_lanes, blk, 0.0)
            ms = jnp.sum(blk * blk, axis=-1, keepdims=True) * (1.0 / FOX_HEAD_DIM)
            blk = blk * lax.rsqrt(ms + NORM_EPS) * (gain[...] * scale)
            blk = blk + aug[:, h * HEAD_PAD:(h + 1) * HEAD_PAD] + ones[...]
            out[0, h] = blk.astype(out.dtype)
    vt_out[0, :, 0] = _vt_tile(u[:, 2 * FOX_WIDTH:], FOX_HEADS)


def _fox_prep(u_fox, u_small, bf, qg, kg, pq, pk, oneq, onek, B, S, ts):
    nb = S // ts
    H = FOX_HEADS
    tok = lambda b, i: (b * nb + i, 0)
    const = lambda b, i: (0, 0)
    return pl.pallas_call(
        _fox_prep_kernel,
        grid=(B, nb),
        in_specs=[pl.BlockSpec((ts, 3 * FOX_WIDTH), tok),
                  pl.BlockSpec((ts, LANES), tok),
                  pl.BlockSpec((1, LANES), const), pl.BlockSpec((1, LANES), const),
                  pl.BlockSpec((1, LANES), const), _vmem_full(), _vmem_full(),
                  pl.BlockSpec((1, LANES), const), pl.BlockSpec((1, LANES), const)],
        out_specs=[pl.BlockSpec((1, H, ts, HEAD_PAD), lambda b, i: (b, 0, i, 0)),
                   pl.BlockSpec((1, H, ts, HEAD_PAD), lambda b, i: (b, 0, i, 0)),
                   pl.BlockSpec((1, H, 1, V_ROWS, ts), lambda b, i: (b, 0, i, 0, 0))],
        out_shape=[jax.ShapeDtypeStruct((B, H, S, HEAD_PAD), BF16),
                   jax.ShapeDtypeStruct((B, H, S, HEAD_PAD), BF16),
                   jax.ShapeDtypeStruct((B, H, nb, V_ROWS, ts), BF16)],
        scratch_shapes=[pltpu.VMEM((1, LANES), F32)],
        compiler_params=_cparams(2),
        name="fox_prep",
    )(u_fox, u_small, bf, qg, kg, pq, pk, oneq, onek)


ATTN_TILE = 512
V_ROWS = 64


def _vt_tile(v_all, heads):
    ts = v_all.shape[0]
    return v_all.T.reshape(heads, V_ROWS, ts).astype(BF16)


HEADS_PER_STEP = 4
BOUND_SLACK = 1.02
BOUND_STEP = 0.25
MAX_SAFE_BOUND = 60.0


def _score_bound(q_gain, k_gain, dim):
    bound = dim * jnp.max(jnp.abs(q_gain)) * jnp.max(jnp.abs(k_gain)) * (dim ** -0.5) * LOG2E * BOUND_SLACK
    bound = jnp.ceil(bound.astype(F32) / BOUND_STEP) * BOUND_STEP
    safe = bound < MAX_SAFE_BOUND
    return jnp.where(safe, bound, 0.0), safe.astype(jnp.int32).reshape(1)


def _attn_kernel(flag_ref, q_ref, k_ref, vt_ref, o_ref, *, tq, chunk_causal):
    i = pl.program_id(2)
    hs = q_ref.shape[1]
    dv = vt_ref.shape[3]

    half = tq // 2

    def causal_mask(shape):
        r = lax.broadcasted_iota(jnp.int32, shape, 0)
        c = lax.broadcasted_iota(jnp.int32, shape, 1)
        shift = CHUNK.bit_length() - 1
        return ((r >> shift) <= (c >> shift)) if chunk_causal else (r <= c)

    def column_sums(p):
        return p.reshape(p.shape[0] // SUBLANES, SUBLANES, p.shape[1]).sum(axis=0)

    def pipeline(items, state):
        state = list(state)
        s_next = items[0][1]()
        for n, (hh, _, consume) in enumerate(items):
            s = s_next
            if n + 1 < len(items):
                s_next = items[n + 1][1]()
            state[hh] = consume(state[hh], s)
        return tuple(state)

    def full_items(js, update):
        items = []
        for j in js:
            for hh in range(hs):
                def scores(j=j, hh=hh):
                    kj = k_ref[0, hh, pl.ds(pl.multiple_of(j * tq, tq), tq), :]
                    return _dot_nt(kj, q_ref[0, hh])
                items.append((hh, scores, lambda st, s, j=j, hh=hh: update(st, s, vt_ref[0, hh, j])))
        return items

    def sweep(update, init, diagonal, finish):
        body = lambda n: (lambda jj, st: pipeline(full_items([n * jj + t for t in range(n)], update), st))
        quads = lax.shift_right_logical(i, 2)
        state = lax.fori_loop(0, quads, body(4), init)
        done = quads * 4
        state = lax.fori_loop(0, lax.shift_right_logical(i - done, 1),
                              lambda _, st: pipeline(full_items([done, done + 1], update), st), state)
        state = lax.fori_loop(0, i & 1, lambda _, st: pipeline(full_items([i - 1], update), st), state)
        state = diagonal(state)
        o = jnp.concatenate([finish(st) for st in state], axis=0)
        o_ref[0] = o.T.astype(o_ref.dtype)

    @pl.when(flag_ref[0] == 1)
    def _():
        def update(st, s, vt):
            l8, acc = st
            p = jnp.exp2(s)
            return l8 + column_sums(p), acc + _dot(vt, p.astype(BF16))

        def diagonal(state):
            ok_a = causal_mask((half, tq))
            ok_b = causal_mask((half, half))
            items = []
            for hh in range(hs):
                def scores_a(hh=hh):
                    ka = k_ref[0, hh, pl.ds(pl.multiple_of(i * tq, tq), half), :]
                    return _dot_nt(ka, q_ref[0, hh])
                def consume_a(st, s, hh=hh):
                    return update(st, jnp.where(ok_a, s, NEG_BIG), vt_ref[0, hh, i, :, :half])
                def scores_b(hh=hh):
                    kb = k_ref[0, hh, pl.ds(pl.multiple_of(i * tq + half, half), half), :]
                    return _dot_nt(kb, q_ref[0, hh, half:, :])
                def consume_b(st, s, hh=hh):
                    l8, acc = st
                    p = jnp.exp2(jnp.where(ok_b, s, NEG_BIG))
                    l8 = jnp.concatenate([l8[:, :half], l8[:, half:] + column_sums(p)], axis=1)
                    pv = _dot(vt_ref[0, hh, i, :, half:], p.astype(BF16))
                    return l8, jnp.concatenate([acc[:, :half], acc[:, half:] + pv], axis=1)
                items += [(hh, scores_a, consume_a), (hh, scores_b, consume_b)]
            return pipeline(items, state)

        init = tuple((jnp.zeros((SUBLANES, tq), F32), jnp.zeros((dv, tq), F32)) for _ in range(hs))
        sweep(update, init, diagonal, lambda st: st[1] * (1.0 / jnp.sum(st[0], axis=0, keepdims=True)))

    @pl.when(flag_ref[0] != 1)
    def _():
        def update(st, s, vt):
            m, l8, acc = st
            m_new = jnp.maximum(m, jnp.max(s, axis=0, keepdims=True))
            alpha = jnp.exp2(m - m_new)
            p = jnp.exp2(s - m_new)
            return m_new, alpha * l8 + column_sums(p), alpha * acc + _dot(vt, p.astype(BF16))

        def diagonal(state):
            ok = causal_mask((tq, tq))
            masked = lambda st, s, vt: update(st, jnp.where(ok, s, NEG_BIG), vt)
            return pipeline(full_items([i], masked), state)

        init = tuple((jnp.full((1, tq), NEG_BIG, F32), jnp.zeros((SUBLANES, tq), F32),
                      jnp.zeros((dv, tq), F32)) for _ in range(hs))
        sweep(update, init, diagonal, lambda st: st[2] * (1.0 / jnp.sum(st[1], axis=0, keepdims=True)))


def _attention(flag, q, k, vt, tq, chunk_causal):
    B, H, S, _ = q.shape
    dv = vt.shape[3]
    hs = HEADS_PER_STEP
    assert vt.shape[4] == tq and (hs * dv) % LANES == 0 and H % hs == 0
    nq = S // tq
    return pl.pallas_call(
        functools.partial(_attn_kernel, tq=tq, chunk_causal=chunk_causal),
        grid=(B, H // hs, nq),
        in_specs=[pl.BlockSpec(memory_space=pltpu.SMEM),
                  pl.BlockSpec((1, hs, tq, HEAD_PAD), lambda b, h, i: (b, h, i, 0)),
                  pl.BlockSpec((1, hs, S, HEAD_PAD), lambda b, h, i: (b, h, 0, 0)),
                  pl.BlockSpec((1, hs, nq, dv, tq), lambda b, h, i: (b, h, 0, 0, 0))],
        out_specs=pl.BlockSpec((1, tq, hs * dv), lambda b, h, i: (b, i, h)),
        out_shape=jax.ShapeDtypeStruct((B, S, H * dv), BF16),
        compiler_params=_cparams(3),
        name="attn_mla" if chunk_causal else "attn_fox",
    )(flag, q, k, vt)


def _ssm_kernel(z_ref, xbc_ref, dt_ref, cw_ref, cb_ref, dtb_ref, alog_ref, dx_ref, ng_ref, e_ref,
                o_ref, xpad, hst, *, lc):
    pad = SUBLANES

    @pl.when(pl.program_id(1) == 0)
    def _():
        xpad[0:pad, :] = jnp.zeros((pad, SSM_XBC), F32)
        hst[...] = jnp.zeros_like(hst)

    xpad[pad:pad + lc, :] = xbc_ref[...].astype(F32)
    xp = xpad[...]
    conv = cb_ref[...] + cw_ref[SSM_CONV - 1:SSM_CONV, :] * xp[pad:, :]
    for back in range(1, SSM_CONV):
        kk = SSM_CONV - 1 - back
        conv = conv + cw_ref[kk:kk + 1, :] * pltpu.roll(xp, back, axis=0)[pad:, :]
    xpad[0:pad, :] = xpad[lc:lc + pad, :]
    xc = _silu(conv)
    xs = xc[:, :SSM_INNER]
    bm = xc[:, SSM_INNER:SSM_INNER + SSM_GROUPS * SSM_STATE]
    cm = xc[:, SSM_INNER + SSM_GROUPS * SSM_STATE:]

    lane = lax.broadcasted_iota(jnp.int32, (lc, LANES), 1)
    dt = jnp.where(lane < SSM_HEADS, _softplus(dt_ref[...] + dtb_ref[...]), 0.0)
    a_neg = -jnp.exp(alog_ref[...]) * LOG2E
    a_cum = _dot3_left(_tril_bf16(lc), dt * a_neg)
    head_lanes = lane < SSM_HEADS
    ea = jnp.where(head_lanes, jnp.exp2(a_cum), 0.0)
    dte = jnp.where(head_lanes, jnp.exp2(a_cum[lc - 1:lc, :] - a_cum), 0.0)
    e01 = e_ref[...]

    def expand(v):
        hi, mid, lo = _split3(v)
        packed = (hi.astype(F32) + pltpu.roll(mid.astype(F32), SSM_HEADS, axis=1)
                  + pltpu.roll(lo.astype(F32), 2 * SSM_HEADS, axis=1)).astype(BF16)
        return _dot(packed, e01)

    dt_x = expand(dt)
    ea_x = expand(ea)
    dte_x = expand(dte)
    xdt = xs * dt_x
    a_t = a_cum.T

    row = lax.broadcasted_iota(jnp.int32, (lc, lc), 0)
    col = lax.broadcasted_iota(jnp.int32, (lc, lc), 1)
    causal = row >= col
    lane_first = lane < SSM_HEAD_DIM
    heads_per_group = SSM_HEADS // SSM_GROUPS
    y_blocks = []
    for g in range(SSM_GROUPS):
        bg = bm[:, g * SSM_STATE:(g + 1) * SSM_STATE]
        cg = cm[:, g * SSM_STATE:(g + 1) * SSM_STATE].astype(BF16)
        cb = _dot_nt(cg, bg.astype(BF16))
        gs = slice(g * SSM_GROUP_WIDTH, (g + 1) * SSM_GROUP_WIDTH)
        h_prev = hst[g]
        y_off = _dot(cg, h_prev.astype(BF16)) * ea_x[:, gs]
        xw = (xdt[:, gs] * dte_x[:, gs]).astype(BF16)
        s_new = _dot(bg.T.astype(BF16), xw)
        hst[g] = h_prev * ea_x[lc - 1:lc, gs] + s_new
        for hp in range(heads_per_group // 2):
            t0 = g * SSM_GROUP_WIDTH + hp * LANES
            x_pair = xdt[:, t0:t0 + LANES]
            acc = None
            for e in range(2):
                h = g * heads_per_group + hp * 2 + e
                seg = a_cum[:, h:h + 1] - a_t[h:h + 1, :]
                decay = jnp.exp2(jnp.where(causal, seg, NEG_BIG))
                gmat = (cb * decay).astype(BF16)
                xh = jnp.where(lane_first if e == 0 else ~lane_first, x_pair, 0.0).astype(BF16)
                part = _dot(gmat, xh)
                acc = part if acc is None else acc + part
            y_blocks.append(acc + y_off[:, hp * LANES:(hp + 1) * LANES])
    y = jnp.concatenate(y_blocks, axis=1) + xs * dx_ref[...]
    zf = z_ref[...].astype(F32)
    yg = y * _silu(zf)
    o_ref[...] = (yg * lax.rsqrt(jnp.mean(yg * yg, axis=-1, keepdims=True) + NORM_EPS)
                  * ng_ref[...]).astype(o_ref.dtype)


def _ssm(u_z, u_xbc, u_small, cw, cb, dtb, alog, dx, ng, e01, B, S, lc):
    nb = S // lc
    tok = lambda b, i: (b * nb + i, 0)
    const = lambda b, i: (0, 0)
    return pl.pallas_call(
        functools.partial(_ssm_kernel, lc=lc),
        grid=(B, nb),
        in_specs=[pl.BlockSpec((lc, SSM_INNER), tok),
                  pl.BlockSpec((lc, SSM_XBC), tok),
                  pl.BlockSpec((lc, LANES), lambda b, i: (b * nb + i, 1)),
                  pl.BlockSpec((SSM_CONV, SSM_XBC), const), pl.BlockSpec((1, SSM_XBC), const),
                  pl.BlockSpec((1, LANES), const), pl.BlockSpec((1, LANES), const),
                  pl.BlockSpec((1, SSM_INNER), const), pl.BlockSpec((1, SSM_INNER), const),
                  _vmem_full()],
        out_specs=pl.BlockSpec((lc, SSM_INNER), tok),
        out_shape=jax.ShapeDtypeStruct((B * S, SSM_INNER), BF16),
        scratch_shapes=[pltpu.VMEM((lc + SUBLANES, SSM_XBC), F32),
                        pltpu.VMEM((SSM_GROUPS, SSM_STATE, SSM_GROUP_WIDTH), F32)],
        compiler_params=_cparams(2),
        name="ssm",
    )(u_z, u_xbc, u_small, cw, cb, dtb, alog, dx, ng, e01)


def _merge_kernel(x_ref, ya_ref, yb_ref, yc_ref, ug_ref, bg_ref, wa_ref, wb_ref, wc_ref, wo_ref, o_ref):
    ys = (_dot(ya_ref[...], wa_ref[...]), _dot(yb_ref[...], wb_ref[...]), _dot(yc_ref[...], wc_ref[...]))
    merged = None
    for n, y in enumerate(ys):
        cs = slice(n * D_MODEL, (n + 1) * D_MODEL)
        gate = _sigmoid(ug_ref[:, cs].astype(F32) + bg_ref[:, cs])
        merged = gate * y if merged is None else merged + gate * y
    o_ref[...] = x_ref[...] + _dot(merged.astype(BF16), wo_ref[...])


def _merge(x2, ya, yb, yc, ug, bg, wa, wb, wc, wo, tm):
    T = x2.shape[0]
    row = lambda w: pl.BlockSpec((tm, w), lambda i: (i, 0))
    return pl.pallas_call(
        _merge_kernel,
        grid=(T // tm,),
        in_specs=[row(D_MODEL), row(MLA_HEADS * MLA_V), row(FOX_WIDTH), row(SSM_INNER), row(3 * D_MODEL),
                  pl.BlockSpec((1, 3 * D_MODEL), lambda i: (0, 0)),
                  _vmem_full(), _vmem_full(), _vmem_full(), _vmem_full()],
        out_specs=row(D_MODEL),
        out_shape=jax.ShapeDtypeStruct((T, D_MODEL), F32),
        compiler_params=_cparams(1),
        name="merge",
    )(x2, ya, yb, yc, ug, bg, wa, wb, wc, wo)


FF_CH = 256


def _ffn_kernel(x_ref, g_ref, wup_ref, cw_ref, cb_ref, wdn_ref, o_ref, abuf, hbuf, *, tm, blocks_per_seq):
    pad = SUBLANES

    @pl.when(pl.program_id(0) % blocks_per_seq == 0)
    def _():
        abuf[0:pad, :] = jnp.zeros((pad, 2 * D_FF), F32)

    x = x_ref[...]
    hn = (x * lax.rsqrt(jnp.mean(x * x, axis=-1, keepdims=True) + NORM_EPS) * g_ref[...]).astype(BF16)
    for c0 in range(0, D_FF, FF_CH):
        halves = []
        for base in (c0, D_FF + c0):
            cs = slice(base, base + FF_CH)
            abuf[pad:pad + tm, cs] = _dot(hn, wup_ref[:, cs])
            ap = abuf[:, cs]
            conv = cb_ref[:, cs] + cw_ref[FFN_CONV - 1:FFN_CONV, cs] * ap[pad:, :]
            for back in range(1, FFN_CONV):
                kk = FFN_CONV - 1 - back
                conv = conv + cw_ref[kk:kk + 1, cs] * pltpu.roll(ap, back, axis=0)[pad:, :]
            abuf[0:pad, cs] = abuf[tm:tm + pad, cs]
            halves.append(conv)
        hbuf[:, c0:c0 + FF_CH] = (_silu(halves[0]) * halves[1]).astype(BF16)
    o_ref[...] = x + _dot(hbuf[...], wdn_ref[...])


def _ffn(x2, g, wup, cw, cb, wdn, S, tm):
    T = x2.shape[0]
    return pl.pallas_call(
        functools.partial(_ffn_kernel, tm=tm, blocks_per_seq=S // tm),
        grid=(T // tm,),
        in_specs=[pl.BlockSpec((tm, D_MODEL), lambda i: (i, 0)),
                  pl.BlockSpec((1, D_MODEL), lambda i: (0, 0)),
                  _vmem_full(),
                  pl.BlockSpec((FFN_CONV, 2 * D_FF), lambda i: (0, 0)),
                  pl.BlockSpec((1, 2 * D_FF), lambda i: (0, 0)),
                  _vmem_full()],
        out_specs=pl.BlockSpec((tm, D_MODEL), lambda i: (i, 0)),
        out_shape=jax.ShapeDtypeStruct((T, D_MODEL), F32),
        scratch_shapes=[pltpu.VMEM((tm + SUBLANES, 2 * D_FF), F32),
                        pltpu.VMEM((tm, D_FF), BF16)],
        compiler_params=_cparams(1),
        name="ffn",
    )(x2, g, wup, cw, cb, wdn)


def _pad_lanes(v, width=LANES):
    return jnp.pad(v.astype(F32), (0, width - v.shape[0])).reshape(1, width)


def _pack_w_in(w):
    zeros = lambda n: jnp.zeros((D_MODEL, n), BF16)
    seg = lambda a, n: w[:, a:a + n].astype(BF16)
    o_fox = MLA_IN
    o_ssm = MLA_IN + FOX_IN
    o_gate = o_ssm + SSM_IN
    kr0 = MLA_Q_RANK + MLA_KV_RANK
    w_mla = jnp.concatenate([seg(0, kr0), zeros(MLA_NOPE), seg(kr0, MLA_ROPE), zeros(LANES - MLA_QK_DIM)],
                            axis=1)
    w_small = jnp.concatenate([seg(o_fox + 3 * FOX_WIDTH, FOX_HEADS), zeros(LANES - FOX_HEADS),
                               seg(o_ssm + SSM_INNER + SSM_XBC, SSM_HEADS), zeros(LANES - SSM_HEADS)], axis=1)
    assert w_mla.shape[1] == MLA_SEG and w_small.shape[1] == SMALL_W
    return [w_mla, seg(o_fox, 3 * FOX_WIDTH), seg(o_ssm, SSM_INNER), seg(o_ssm + SSM_INNER, SSM_XBC),
            seg(o_gate, 3 * D_MODEL), w_small]


def _one_hot_row(lanes, value=1.0):
    row = np.zeros((1, LANES), np.float32)
    row[0, lanes] = value
    return row


def _fox_placement():
    rows = np.arange(3 * FOX_HEADS)
    piece = rows // FOX_HEADS
    head = rows % FOX_HEADS
    pq = np.zeros((LANES, FOX_HEADS * HEAD_PAD), np.float32)
    pk = np.zeros((LANES, FOX_HEADS * HEAD_PAD), np.float32)
    pq[rows, head * HEAD_PAD + AUG0 + piece] = 1.0
    pk[rows, head * HEAD_PAD + AUG0 + 3 + piece] = -1.0
    oneq = _one_hot_row(slice(AUG0 + 3, AUG0 + 6))
    onek = _one_hot_row(slice(AUG0, AUG0 + 3))
    return jnp.asarray(pq, BF16), jnp.asarray(pk, BF16), jnp.asarray(oneq), jnp.asarray(onek)


def _shift_rows(bound, lane):
    return -bound * jnp.asarray(_one_hot_row(lane)), jnp.asarray(_one_hot_row(lane))


def _head_expand():
    h = np.arange(3 * SSM_HEADS)
    e = np.zeros((LANES, SSM_INNER), np.float32)
    cols = (h % SSM_HEADS)[:, None] * SSM_HEAD_DIM + np.arange(SSM_HEAD_DIM)[None, :]
    e[h[:, None], cols] = 1.0
    return jnp.asarray(e, BF16)


def kernel(x, positions, norm_mix_g, w_in, b_gate, mla_q_norm_g, mla_w_uq, mla_kv_norm_g, mla_w_ukv,
           mla_q_gain, mla_k_gain, fox_q_gain, fox_k_gain, fox_b_f, ssm_conv_w, ssm_conv_b, ssm_dt_bias,
           ssm_A_log, ssm_D, ssm_norm_g, w_br_mla, w_br_fox, w_br_ssm, w_out, norm_ffn_g, ffn_w_up,
           ffn_conv_w, ffn_conv_b, ffn_w_down):
    B, S, D = x.shape
    T = B * S
    depth = w_in.shape[0]
    tile = min(256, S)
    td = min(DENSE_TILE, S)
    ta = min(ATTN_TILE, S)

    pos = positions.reshape(T, 1).astype(F32)
    cos, sa, sb = _rope_tables(pos, tile)
    pq, pk, oneq, onek = _fox_placement()
    e01 = _head_expand()
    row = lambda v: v.reshape(1, -1).astype(F32)

    x2 = x.reshape(T, D).astype(F32)
    for l in range(depth):
        u_mla, u_fox, u_z, u_xbc, u_gate, u_small = _inproj(x2, row(norm_mix_g[l]), _pack_w_in(w_in[l]), td)

        wq = jnp.pad(mla_w_uq[l].reshape(MLA_Q_RANK, MLA_HEADS, MLA_QK_DIM),
                     ((0, 0), (0, 0), (0, HEAD_PAD - MLA_QK_DIM))).reshape(MLA_Q_RANK, -1).astype(BF16)
        wkv = mla_w_ukv[l].reshape(MLA_KV_RANK, MLA_HEADS, MLA_NOPE + MLA_V)
        wk = jnp.pad(wkv[:, :, :MLA_NOPE], ((0, 0), (0, 0), (0, HEAD_PAD - MLA_NOPE))
                     ).reshape(MLA_KV_RANK, -1).astype(BF16)
        wv = wkv[:, :, MLA_NOPE:].reshape(MLA_KV_RANK, -1).astype(BF16)
        bound, flag = _score_bound(mla_q_gain[l], mla_k_gain[l], MLA_QK_DIM)
        qfix, kfix = _shift_rows(bound, MLA_QK_DIM)
        q, k, vt = _mla_prep(u_mla, cos, sa, sb, row(mla_q_norm_g[l]), wq, row(mla_kv_norm_g[l]), wk, wv,
                             _pad_lanes(mla_q_gain[l]), _pad_lanes(mla_k_gain[l]), qfix, kfix, B, S, ta)
        y_a = _attention(flag, q, k, vt, ta, True).reshape(T, -1)

        bound, flag = _score_bound(fox_q_gain[l], fox_k_gain[l], FOX_HEAD_DIM)
        qfix, kfix = _shift_rows(bound, AUG0 + 6)
        qf, kf, vtf = _fox_prep(u_fox, u_small, _pad_lanes(fox_b_f[l]), _pad_lanes(fox_q_gain[l]),
                                _pad_lanes(fox_k_gain[l]), pq, pk, oneq + qfix, onek + kfix, B, S, ta)
        y_b = _attention(flag, qf, kf, vtf, ta, False).reshape(T, -1)

        dx = jnp.repeat(ssm_D[l].astype(F32), SSM_HEAD_DIM).reshape(1, -1)
        y_c = _ssm(u_z, u_xbc, u_small, ssm_conv_w[l].astype(F32), row(ssm_conv_b[l]),
                   _pad_lanes(ssm_dt_bias[l]), _pad_lanes(ssm_A_log[l]), dx, row(ssm_norm_g[l]), e01,
                   B, S, tile)

        x2 = _merge(x2, y_a, y_b, y_c, u_gate, row(b_gate[l]), w_br_mla[l].astype(BF16),
                    w_br_fox[l].astype(BF16), w_br_ssm[l].astype(BF16), w_out[l].astype(BF16), td)
        x2 = _ffn(x2, row(norm_ffn_g[l]), ffn_w_up[l].astype(BF16), ffn_conv_w[l].astype(F32),
                  row(ffn_conv_b[l]), ffn_w_down[l].astype(BF16), S, td)
    return x2.reshape(B, S, D).astype(x.dtype)
```

```python
import functools

import numpy as np
import jax
import jax.numpy as jnp
from jax import lax
from jax.experimental import pallas as pl
from jax.experimental.pallas import tpu as pltpu

F32 = jnp.float32
BF16 = jnp.bfloat16

D_MODEL = 1024
NORM_EPS = 1e-6
CHUNK = 64

MLA_HEADS = 8
MLA_Q_RANK = 384
MLA_KV_RANK = 256
MLA_NOPE = 64
MLA_ROPE = 32
MLA_QK_DIM = MLA_NOPE + MLA_ROPE
MLA_V = 64
ROPE_THETA = 10000.0

FOX_HEADS = 8
FOX_HEAD_DIM = 64
FOX_WIDTH = FOX_HEADS * FOX_HEAD_DIM

SSM_HEADS = 16
SSM_HEAD_DIM = 64
SSM_INNER = SSM_HEADS * SSM_HEAD_DIM
SSM_GROUPS = 2
SSM_STATE = 128
SSM_CONV = 4
SSM_XBC = SSM_INNER + 2 * SSM_GROUPS * SSM_STATE
SSM_GROUP_WIDTH = SSM_INNER // SSM_GROUPS

D_FF = 2816
FFN_CONV = 3

MLA_IN = MLA_Q_RANK + MLA_KV_RANK + MLA_ROPE
FOX_IN = 3 * FOX_WIDTH + FOX_HEADS
SSM_IN = SSM_INNER + SSM_XBC + SSM_HEADS

LANES = 128
SUBLANES = 8
HEAD_PAD = LANES
LOG2E = 1.4426950408889634
NEG_BIG = -1e30
VMEM_LIMIT = 52 * 1024 * 1024

MLA_SEG = MLA_Q_RANK + MLA_KV_RANK + LANES
SMALL_W = 2 * LANES

AUG0 = FOX_HEAD_DIM


def _cparams(n_grid):
    return pltpu.CompilerParams(dimension_semantics=("arbitrary",) * n_grid,
                                vmem_limit_bytes=VMEM_LIMIT)


def _vmem_full():
    return pl.BlockSpec(memory_space=pltpu.VMEM)


def _dot(a, b):
    return jnp.dot(a, b, preferred_element_type=F32)


def _dot_nt(a, b):
    return lax.dot_general(a, b, (((1,), (1,)), ((), ())), preferred_element_type=F32)


def _split3(x):
    hi = x.astype(BF16)
    r = x - hi.astype(F32)
    mid = r.astype(BF16)
    lo = (r - mid.astype(F32)).astype(BF16)
    return hi, mid, lo


def _dot3_left(m01, x):
    hi, mid, lo = _split3(x)
    return _dot(m01, hi) + _dot(m01, mid) + _dot(m01, lo)


def _dot3_right(x, m01):
    hi, mid, lo = _split3(x)
    return _dot(hi, m01) + _dot(mid, m01) + _dot(lo, m01)


def _sigmoid(x):
    return 1.0 / (1.0 + jnp.exp(-x))


def _silu(x):
    return x * _sigmoid(x)


def _softplus(x):
    return jnp.maximum(x, 0.0) + jnp.log(1.0 + jnp.exp(-jnp.abs(x)))


def _tril_bf16(n):
    r = lax.broadcasted_iota(jnp.int32, (n, n), 0)
    c = lax.broadcasted_iota(jnp.int32, (n, n), 1)
    return jnp.where(r >= c, 1.0, 0.0).astype(BF16)


def _rope_kernel(pos_ref, inv_ref, cos_ref, sa_ref, sb_ref):
    ang = pos_ref[...] * inv_ref[...]
    lane = lax.broadcasted_iota(jnp.int32, ang.shape, 1)
    c = jnp.cos(ang)
    s = jnp.sin(ang)
    half = MLA_ROPE // 2
    first = (lane >= MLA_NOPE) & (lane < MLA_NOPE + half)
    second = (lane >= MLA_NOPE + half) & (lane < MLA_QK_DIM)
    cos_ref[...] = jnp.where(lane < MLA_NOPE, 1.0, jnp.where(first | second, c, 0.0))
    sa_ref[...] = jnp.where(first, -s, 0.0)
    sb_ref[...] = jnp.where(second, s, 0.0)


def _rope_tables(pos, ts):
    T = pos.shape[0]
    half = MLA_ROPE // 2
    inv = 1.0 / (ROPE_THETA ** (jnp.arange(0, MLA_ROPE, 2, dtype=F32) / MLA_ROPE))
    inv_lane = jnp.concatenate([jnp.zeros((MLA_NOPE,), F32), inv, inv,
                                jnp.zeros((LANES - MLA_QK_DIM,), F32)]).reshape(1, LANES)
    tab = jax.ShapeDtypeStruct((T, LANES), F32)
    spec = pl.BlockSpec((ts, LANES), lambda i: (i, 0))
    return pl.pallas_call(
        _rope_kernel,
        grid=(T // ts,),
        in_specs=[pl.BlockSpec((ts, 1), lambda i: (i, 0)), pl.BlockSpec((1, LANES), lambda i: (0, 0))],
        out_specs=[spec, spec, spec],
        out_shape=[tab, tab, tab],
        compiler_params=_cparams(1),
        name="rope_tables",
    )(pos, inv_lane)


def _rope(x, cos, sa, sb):
    half = MLA_ROPE // 2
    return x * cos + pltpu.roll(x, LANES - half, axis=1) * sa + pltpu.roll(x, half, axis=1) * sb


IN_CH = 512
DENSE_TILE = 512


def _mla_prep(u, cos_ref, sa_ref, sb_ref, gq_ref, wq_ref, gkv_ref, wk_ref, wv_ref,
              qg_ref, kg_ref, qfix_ref, kfix_ref, q_out, k_out, vt_out):
    cos = cos_ref[...]
    sa = sa_ref[...]
    sb = sb_ref[...]
    cq = u[:, :MLA_Q_RANK]
    ckv = u[:, MLA_Q_RANK:MLA_Q_RANK + MLA_KV_RANK]
    kr = u[:, MLA_Q_RANK + MLA_KV_RANK:]
    cqn = (cq * lax.rsqrt(jnp.mean(cq * cq, axis=-1, keepdims=True) + NORM_EPS) * gq_ref[...]).astype(BF16)
    ckvn = (ckv * lax.rsqrt(jnp.mean(ckv * ckv, axis=-1, keepdims=True) + NORM_EPS) * gkv_ref[...]).astype(BF16)
    q_all = _dot(cqn, wq_ref[...])
    k_all = _dot(ckvn, wk_ref[...])
    v_all = _dot(ckvn, wv_ref[...])
    kr_ss = jnp.sum(kr * kr, axis=-1, keepdims=True)
    kr_rot = _rope(kr * kg_ref[...], cos, sa, sb)
    q_scale = (MLA_QK_DIM ** -0.5) * LOG2E
    for h in range(MLA_HEADS):
        qh = q_all[:, h * HEAD_PAD:(h + 1) * HEAD_PAD]
        ms = jnp.sum(qh * qh, axis=-1, keepdims=True) * (1.0 / MLA_QK_DIM)
        qh = qh * lax.rsqrt(ms + NORM_EPS) * qg_ref[...]
        q_out[0, h] = (_rope(qh, cos, sa, sb) * q_scale + qfix_ref[...]).astype(q_out.dtype)
        kn = k_all[:, h * HEAD_PAD:(h + 1) * HEAD_PAD]
        ms = (jnp.sum(kn * kn, axis=-1, keepdims=True) + kr_ss) * (1.0 / MLA_QK_DIM)
        kh = (kn * kg_ref[...] + kr_rot) * lax.rsqrt(ms + NORM_EPS)
        k_out[0, h] = (kh + kfix_ref[...]).astype(k_out.dtype)
    vt_out[0, :, 0] = _vt_tile(v_all, MLA_HEADS)


def _fox_prep(u, f_raw, bf_ref, qg_ref, kg_ref, pq_ref, pk_ref, oneq_ref, onek_ref,
              q_out, k_out, vt_out, carry):
    ts = u.shape[0]
    lane = lax.broadcasted_iota(jnp.int32, (ts, LANES), 1)
    f = f_raw + bf_ref[...]
    log_f = jnp.minimum(f, 0.0) - jnp.log(1.0 + jnp.exp(-jnp.abs(f)))
    log_f = jnp.where(lane < FOX_HEADS, log_f * LOG2E, 0.0)
    cum = carry[...] + _dot3_left(_tril_bf16(ts), log_f)
    carry[...] = cum[ts - 1:ts, :]
    hi, mid, lo = _split3(cum)
    packed = (hi.astype(F32) + pltpu.roll(mid.astype(F32), FOX_HEADS, axis=1)
              + pltpu.roll(lo.astype(F32), 2 * FOX_HEADS, axis=1)).astype(BF16)
    aug_q = _dot(packed, pq_ref[...])
    aug_k = _dot(packed, pk_ref[...])

    q_scale = (FOX_HEAD_DIM ** -0.5) * LOG2E
    head_lanes = lane < FOX_HEAD_DIM
    for h in range(FOX_HEADS):
        t0 = (h // 2) * LANES
        for src, gain, aug, ones, scale, out in (
                (0, qg_ref, aug_q, oneq_ref, q_scale, q_out),
                (FOX_WIDTH, kg_ref, aug_k, onek_ref, 1.0, k_out)):
            blk = u[:, src + t0:src + t0 + LANES]
            if h % 2 == 1:
                blk = pltpu.roll(blk, FOX_HEAD_DIM, axis=1)
            blk = jnp.where(head_lanes, blk, 0.0)
            ms = jnp.sum(blk * blk, axis=-1, keepdims=True) * (1.0 / FOX_HEAD_DIM)
            blk = blk * lax.rsqrt(ms + NORM_EPS) * (gain[...] * scale)
            blk = blk + aug[:, h * HEAD_PAD:(h + 1) * HEAD_PAD] + ones[...]
            out[0, h] = blk.astype(out.dtype)
    vt_out[0, :, 0] = _vt_tile(u[:, 2 * FOX_WIDTH:], FOX_HEADS)


N_SEG = 6
N_MLA = 12
N_FOX = 7


def _front_kernel(x_ref, g_ref, *refs):
    w_mla, w_fox, w_small, w_z, w_xbc, w_gate = refs[:N_SEG]
    mla_in = refs[N_SEG:N_SEG + N_MLA]
    fox_in = refs[N_SEG + N_MLA:N_SEG + N_MLA + N_FOX]
    outs = refs[N_SEG + N_MLA + N_FOX:]
    q_out, k_out, vt_out, qf_out, kf_out, vtf_out, dt_out, z_out, xbc_out, gate_out, carry = outs

    @pl.when(pl.program_id(1) == 0)
    def _():
        carry[...] = jnp.zeros_like(carry)

    x = x_ref[...]
    hn = (x * lax.rsqrt(jnp.mean(x * x, axis=-1, keepdims=True) + NORM_EPS) * g_ref[...]).astype(BF16)
    u_mla = _dot(hn, w_mla[...])
    u_fox = _dot(hn, w_fox[...])
    u_small = _dot(hn, w_small[...])
    dt_out[...] = u_small[:, LANES:]
    _mla_prep(u_mla, *mla_in, q_out, k_out, vt_out)
    _fox_prep(u_fox, u_small[:, :LANES], *fox_in, qf_out, kf_out, vtf_out, carry)
    for w_ref, o_ref in ((w_z, z_out), (w_xbc, xbc_out), (w_gate, gate_out)):
        n = w_ref.shape[1]
        for cc in range(0, n, IN_CH):
            w = min(IN_CH, n - cc)
            o_ref[:, cc:cc + w] = _dot(hn, w_ref[:, cc:cc + w]).astype(o_ref.dtype)


def _front(x2, g, weights, mla_in, fox_in, B, S, ts, tk):
    nb = S // ts
    T = B * S
    H = MLA_HEADS
    per_tile = tk // ts
    assert len(weights) == N_SEG and len(mla_in) == N_MLA and len(fox_in) == N_FOX and tk % ts == 0
    tok = lambda b, i: (b * nb + i, 0)
    const = lambda b, i: (0, 0)
    row = lambda w: pl.BlockSpec((1, w), const)
    tab = pl.BlockSpec((ts, LANES), tok)
    head_major = pl.BlockSpec((1, H, ts, HEAD_PAD), lambda b, i: (b, 0, i, 0))
    vt_spec = pl.BlockSpec((1, H, 1, V_ROWS, ts), lambda b, i: (b, 0, i // per_tile, 0, i % per_tile))
    qk_shape = jax.ShapeDtypeStruct((B, H, S, HEAD_PAD), BF16)
    vt_shape = jax.ShapeDtypeStruct((B, H, S // tk, V_ROWS, tk), BF16)
    wide = lambda n: (pl.BlockSpec((ts, n), tok), jax.ShapeDtypeStruct((T, n), BF16))
    z_spec, z_shape = wide(SSM_INNER)
    xbc_spec, xbc_shape = wide(SSM_XBC)
    gate_spec, gate_shape = wide(3 * D_MODEL)
    return pl.pallas_call(
        _front_kernel,
        grid=(B, nb),
        in_specs=[pl.BlockSpec((ts, D_MODEL), tok), row(D_MODEL)] + [_vmem_full()] * N_SEG
                 + [tab, tab, tab, row(MLA_Q_RANK), _vmem_full(), row(MLA_KV_RANK), _vmem_full(), _vmem_full(),
                    row(LANES), row(LANES), row(LANES), row(LANES)]
                 + [row(LANES), row(LANES), row(LANES), _vmem_full(), _vmem_full(), row(LANES), row(LANES)],
        out_specs=[head_major, head_major, vt_spec, head_major, head_major, vt_spec,
                   tab, z_spec, xbc_spec, gate_spec],
        out_shape=[qk_shape, qk_shape, vt_shape, qk_shape, qk_shape, vt_shape,
                   jax.ShapeDtypeStruct((T, LANES), F32), z_shape, xbc_shape, gate_shape],
        scratch_shapes=[pltpu.VMEM((1, LANES), F32)],
        compiler_params=_cparams(2),
        name="front",
    )(x2, g, *weights, *mla_in, *fox_in)


ATTN_TILE = 512
V_ROWS = 64


def _vt_tile(v_all, heads):
    ts = v_all.shape[0]
    return v_all.T.reshape(heads, V_ROWS, ts).astype(BF16)


HEADS_PER_STEP = 4
BOUND_SLACK = 1.02
BOUND_STEP = 0.25
MAX_SAFE_BOUND = 60.0


def _score_bound(q_gain, k_gain, dim):
    bound = dim * jnp.max(jnp.abs(q_gain)) * jnp.max(jnp.abs(k_gain)) * (dim ** -0.5) * LOG2E * BOUND_SLACK
    bound = jnp.ceil(bound.astype(F32) / BOUND_STEP) * BOUND_STEP
    safe = bound < MAX_SAFE_BOUND
    return jnp.where(safe, bound, 0.0), safe.astype(jnp.int32).reshape(1)


def _attn_kernel(flag_ref, q_ref, k_ref, vt_ref, o_ref, *, tq, chunk_causal):
    i = pl.program_id(2)
    hs = q_ref.shape[1]
    dv = vt_ref.shape[3]

    half = tq // 2

    def causal_mask(shape):
        r = lax.broadcasted_iota(jnp.int32, shape, 0)
        c = lax.broadcasted_iota(jnp.int32, shape, 1)
        shift = CHUNK.bit_length() - 1
        return ((r >> shift) <= (c >> shift)) if chunk_causal else (r <= c)

    def column_sums(p):
        return p.reshape(p.shape[0] // SUBLANES, SUBLANES, p.shape[1]).sum(axis=0)

    def pipeline(items, state):
        state = list(state)
        s_next = items[0][1]()
        for n, (hh, _, consume) in enumerate(items):
            s = s_next
            if n + 1 < len(items):
                s_next = items[n + 1][1]()
            state[hh] = consume(state[hh], s)
        return tuple(state)

    def full_items(js, update):
        items = []
        for j in js:
            for hh in range(hs):
                def scores(j=j, hh=hh):
                    kj = k_ref[0, hh, pl.ds(pl.multiple_of(j * tq, tq), tq), :]
                    return _dot_nt(kj, q_ref[0, hh])
                items.append((hh, scores, lambda st, s, j=j, hh=hh: update(st, s, vt_ref[0, hh, j])))
        return items

    def sweep(update, init, diagonal, finish):
        body = lambda n: (lambda jj, st: pipeline(full_items([n * jj + t for t in range(n)], update), st))
        quads = lax.shift_right_logical(i, 2)
        state = lax.fori_loop(0, quads, body(4), init)
        done = quads * 4
        state = lax.fori_loop(0, lax.shift_right_logical(i - done, 1),
                              lambda _, st: pipeline(full_items([done, done + 1], update), st), state)
        state = lax.fori_loop(0, i & 1, lambda _, st: pipeline(full_items([i - 1], update), st), state)
        state = diagonal(state)
        o = jnp.concatenate([finish(st) for st in state], axis=0)
        o_ref[0] = o.T.astype(o_ref.dtype)

    @pl.when(flag_ref[0] == 1)
    def _():
        def update(st, s, vt):
            l8, acc = st
            p = jnp.exp2(s)
            return l8 + column_sums(p), acc + _dot(vt, p.astype(BF16))

        def diagonal(state):
            ok_a = causal_mask((half, tq))
            ok_b = causal_mask((half, half))
            items = []
            for hh in range(hs):
                def scores_a(hh=hh):
                    ka = k_ref[0, hh, pl.ds(pl.multiple_of(i * tq, tq), half), :]
                    return _dot_nt(ka, q_ref[0, hh])
                def consume_a(st, s, hh=hh):
                    return update(st, jnp.where(ok_a, s, NEG_BIG), vt_ref[0, hh, i, :, :half])
                def scores_b(hh=hh):
                    kb = k_ref[0, hh, pl.ds(pl.multiple_of(i * tq + half, half), half), :]
                    return _dot_nt(kb, q_ref[0, hh, half:, :])
                def consume_b(st, s, hh=hh):
                    l8, acc = st
                    p = jnp.exp2(jnp.where(ok_b, s, NEG_BIG))
                    l8 = jnp.concatenate([l8[:, :half], l8[:, half:] + column_sums(p)], axis=1)
                    pv = _dot(vt_ref[0, hh, i, :, half:], p.astype(BF16))
                    return l8, jnp.concatenate([acc[:, :half], acc[:, half:] + pv], axis=1)
                items += [(hh, scores_a, consume_a), (hh, scores_b, consume_b)]
            return pipeline(items, state)

        init = tuple((jnp.zeros((SUBLANES, tq), F32), jnp.zeros((dv, tq), F32)) for _ in range(hs))
        sweep(update, init, diagonal, lambda st: st[1] * (1.0 / jnp.sum(st[0], axis=0, keepdims=True)))

    @pl.when(flag_ref[0] != 1)
    def _():
        def update(st, s, vt):
            m, l8, acc = st
            m_new = jnp.maximum(m, jnp.max(s, axis=0, keepdims=True))
            alpha = jnp.exp2(m - m_new)
            p = jnp.exp2(s - m_new)
            return m_new, alpha * l8 + column_sums(p), alpha * acc + _dot(vt, p.astype(BF16))

        def diagonal(state):
            ok = causal_mask((tq, tq))
            masked = lambda st, s, vt: update(st, jnp.where(ok, s, NEG_BIG), vt)
            return pipeline(full_items([i], masked), state)

        init = tuple((jnp.full((1, tq), NEG_BIG, F32), jnp.zeros((SUBLANES, tq), F32),
                      jnp.zeros((dv, tq), F32)) for _ in range(hs))
        sweep(update, init, diagonal, lambda st: st[2] * (1.0 / jnp.sum(st[1], axis=0, keepdims=True)))


def _attention(flag, q, k, vt, tq, chunk_causal):
    B, H, S, _ = q.shape
    dv = vt.shape[3]
    hs = HEADS_PER_STEP
    assert vt.shape[4] == tq and (hs * dv) % LANES == 0 and H % hs == 0
    nq = S // tq
    return pl.pallas_call(
        functools.partial(_attn_kernel, tq=tq, chunk_causal=chunk_causal),
        grid=(B, H // hs, nq),
        in_specs=[pl.BlockSpec(memory_space=pltpu.SMEM),
                  pl.BlockSpec((1, hs, tq, HEAD_PAD), lambda b, h, i: (b, h, i, 0)),
                  pl.BlockSpec((1, hs, S, HEAD_PAD), lambda b, h, i: (b, h, 0, 0)),
                  pl.BlockSpec((1, hs, nq, dv, tq), lambda b, h, i: (b, h, 0, 0, 0))],
        out_specs=pl.BlockSpec((1, tq, hs * dv), lambda b, h, i: (b, i, h)),
        out_shape=jax.ShapeDtypeStruct((B, S, H * dv), BF16),
        compiler_params=_cparams(3),
        name="attn_mla" if chunk_causal else "attn_fox",
    )(flag, q, k, vt)


def _ssm_kernel(z_ref, xbc_ref, dt_ref, cw_ref, cb_ref, dtb_ref, alog_ref, dx_ref, ng_ref, e_ref,
                o_ref, xpad, hst, *, lc):
    pad = SUBLANES

    @pl.when(pl.program_id(1) == 0)
    def _():
        xpad[0:pad, :] = jnp.zeros((pad, SSM_XBC), F32)
        hst[...] = jnp.zeros_like(hst)

    xpad[pad:pad + lc, :] = xbc_ref[...].astype(F32)
    xp = xpad[...]
    conv = cb_ref[...] + cw_ref[SSM_CONV - 1:SSM_CONV, :] * xp[pad:, :]
    for back in range(1, SSM_CONV):
        kk = SSM_CONV - 1 - back
        conv = conv + cw_ref[kk:kk + 1, :] * pltpu.roll(xp, back, axis=0)[pad:, :]
    xpad[0:pad, :] = xpad[lc:lc + pad, :]
    xc = _silu(conv)
    xs = xc[:, :SSM_INNER]
    bm = xc[:, SSM_INNER:SSM_INNER + SSM_GROUPS * SSM_STATE]
    cm = xc[:, SSM_INNER + SSM_GROUPS * SSM_STATE:]

    lane = lax.broadcasted_iota(jnp.int32, (lc, LANES), 1)
    dt = jnp.where(lane < SSM_HEADS, _softplus(dt_ref[...] + dtb_ref[...]), 0.0)
    a_neg = -jnp.exp(alog_ref[...]) * LOG2E
    a_cum = _dot3_left(_tril_bf16(lc), dt * a_neg)
    head_lanes = lane < SSM_HEADS
    ea = jnp.where(head_lanes, jnp.exp2(a_cum), 0.0)
    dte = jnp.where(head_lanes, jnp.exp2(a_cum[lc - 1:lc, :] - a_cum), 0.0)
    e01 = e_ref[...]

    def expand(v):
        hi, mid, lo = _split3(v)
        packed = (hi.astype(F32) + pltpu.roll(mid.astype(F32), SSM_HEADS, axis=1)
                  + pltpu.roll(lo.astype(F32), 2 * SSM_HEADS, axis=1)).astype(BF16)
        return _dot(packed, e01)

    dt_x = expand(dt)
    ea_x = expand(ea)
    dte_x = expand(dte)
    xdt = xs * dt_x
    a_t = a_cum.T

    row = lax.broadcasted_iota(jnp.int32, (lc, lc), 0)
    col = lax.broadcasted_iota(jnp.int32, (lc, lc), 1)
    causal = row >= col
    lane_first = lane < SSM_HEAD_DIM
    heads_per_group = SSM_HEADS // SSM_GROUPS
    y_blocks = []
    for g in range(SSM_GROUPS):
        bg = bm[:, g * SSM_STATE:(g + 1) * SSM_STATE]
        cg = cm[:, g * SSM_STATE:(g + 1) * SSM_STATE].astype(BF16)
        cb = _dot_nt(cg, bg.astype(BF16))
        gs = slice(g * SSM_GROUP_WIDTH, (g + 1) * SSM_GROUP_WIDTH)
        h_prev = hst[g]
        y_off = _dot(cg, h_prev.astype(BF16)) * ea_x[:, gs]
        xw = (xdt[:, gs] * dte_x[:, gs]).astype(BF16)
        s_new = _dot(bg.T.astype(BF16), xw)
        hst[g] = h_prev * ea_x[lc - 1:lc, gs] + s_new
        for hp in range(heads_per_group // 2):
            t0 = g * SSM_GROUP_WIDTH + hp * LANES
            x_pair = xdt[:, t0:t0 + LANES]
            acc = None
            for e in range(2):
                h = g * heads_per_group + hp * 2 + e
                seg = a_cum[:, h:h + 1] - a_t[h:h + 1, :]
                decay = jnp.exp2(jnp.where(causal, seg, NEG_BIG))
                gmat = (cb * decay).astype(BF16)
                xh = jnp.where(lane_first if e == 0 else ~lane_first, x_pair, 0.0).astype(BF16)
                part = _dot(gmat, xh)
                acc = part if acc is None else acc + part
            y_blocks.append(acc + y_off[:, hp * LANES:(hp + 1) * LANES])
    y = jnp.concatenate(y_blocks, axis=1) + xs * dx_ref[...]
    zf = z_ref[...].astype(F32)
    yg = y * _silu(zf)
    o_ref[...] = (yg * lax.rsqrt(jnp.mean(yg * yg, axis=-1, keepdims=True) + NORM_EPS)
                  * ng_ref[...]).astype(o_ref.dtype)


def _ssm(u_z, u_xbc, u_small, cw, cb, dtb, alog, dx, ng, e01, B, S, lc):
    nb = S // lc
    tok = lambda b, i: (b * nb + i, 0)
    const = lambda b, i: (0, 0)
    return pl.pallas_call(
        functools.partial(_ssm_kernel, lc=lc),
        grid=(B, nb),
        in_specs=[pl.BlockSpec((lc, SSM_INNER), tok),
                  pl.BlockSpec((lc, SSM_XBC), tok),
                  pl.BlockSpec((lc, LANES), tok),
                  pl.BlockSpec((SSM_CONV, SSM_XBC), const), pl.BlockSpec((1, SSM_XBC), const),
                  pl.BlockSpec((1, LANES), const), pl.BlockSpec((1, LANES), const),
                  pl.BlockSpec((1, SSM_INNER), const), pl.BlockSpec((1, SSM_INNER), const),
                  _vmem_full()],
        out_specs=pl.BlockSpec((lc, SSM_INNER), tok),
        out_shape=jax.ShapeDtypeStruct((B * S, SSM_INNER), BF16),
        scratch_shapes=[pltpu.VMEM((lc + SUBLANES, SSM_XBC), F32),
                        pltpu.VMEM((SSM_GROUPS, SSM_STATE, SSM_GROUP_WIDTH), F32)],
        compiler_params=_cparams(2),
        name="ssm",
    )(u_z, u_xbc, u_small, cw, cb, dtb, alog, dx, ng, e01)


def _merge_kernel(x_ref, ya_ref, yb_ref, yc_ref, ug_ref, bg_ref, wa_ref, wb_ref, wc_ref, wo_ref, o_ref):
    ys = (_dot(ya_ref[...], wa_ref[...]), _dot(yb_ref[...], wb_ref[...]), _dot(yc_ref[...], wc_ref[...]))
    merged = None
    for n, y in enumerate(ys):
        cs = slice(n * D_MODEL, (n + 1) * D_MODEL)
        gate = _sigmoid(ug_ref[:, cs].astype(F32) + bg_ref[:, cs])
        merged = gate * y if merged is None else merged + gate * y
    o_ref[...] = x_ref[...] + _dot(merged.astype(BF16), wo_ref[...])


def _merge(x2, ya, yb, yc, ug, bg, wa, wb, wc, wo, tm):
    T = x2.shape[0]
    row = lambda w: pl.BlockSpec((tm, w), lambda i: (i, 0))
    return pl.pallas_call(
        _merge_kernel,
        grid=(T // tm,),
        in_specs=[row(D_MODEL), row(MLA_HEADS * MLA_V), row(FOX_WIDTH), row(SSM_INNER), row(3 * D_MODEL),
                  pl.BlockSpec((1, 3 * D_MODEL), lambda i: (0, 0)),
                  _vmem_full(), _vmem_full(), _vmem_full(), _vmem_full()],
        out_specs=row(D_MODEL),
        out_shape=jax.ShapeDtypeStruct((T, D_MODEL), F32),
        compiler_params=_cparams(1),
        name="merge",
    )(x2, ya, yb, yc, ug, bg, wa, wb, wc, wo)


FF_CH = 256


def _ffn_kernel(x_ref, g_ref, wup_ref, cw_ref, cb_ref, wdn_ref, o_ref, abuf, hbuf, *, tm, blocks_per_seq):
    pad = SUBLANES

    @pl.when(pl.program_id(0) % blocks_per_seq == 0)
    def _():
        abuf[0:pad, :] = jnp.zeros((pad, 2 * D_FF), F32)

    x = x_ref[...]
    hn = (x * lax.rsqrt(jnp.mean(x * x, axis=-1, keepdims=True) + NORM_EPS) * g_ref[...]).astype(BF16)
    for c0 in range(0, D_FF, FF_CH):
        halves = []
        for base in (c0, D_FF + c0):
            cs = slice(base, base + FF_CH)
            abuf[pad:pad + tm, cs] = _dot(hn, wup_ref[:, cs])
            ap = abuf[:, cs]
            conv = cb_ref[:, cs] + cw_ref[FFN_CONV - 1:FFN_CONV, cs] * ap[pad:, :]
            for back in range(1, FFN_CONV):
                kk = FFN_CONV - 1 - back
                conv = conv + cw_ref[kk:kk + 1, cs] * pltpu.roll(ap, back, axis=0)[pad:, :]
            abuf[0:pad, cs] = abuf[tm:tm + pad, cs]
            halves.append(conv)
        hbuf[:, c0:c0 + FF_CH] = (_silu(halves[0]) * halves[1]).astype(BF16)
    o_ref[...] = x + _dot(hbuf[...], wdn_ref[...])


def _ffn(x2, g, wup, cw, cb, wdn, S, tm):
    T = x2.shape[0]
    return pl.pallas_call(
        functools.partial(_ffn_kernel, tm=tm, blocks_per_seq=S // tm),
        grid=(T // tm,),
        in_specs=[pl.BlockSpec((tm, D_MODEL), lambda i: (i, 0)),
                  pl.BlockSpec((1, D_MODEL), lambda i: (0, 0)),
                  _vmem_full(),
                  pl.BlockSpec((FFN_CONV, 2 * D_FF), lambda i: (0, 0)),
                  pl.BlockSpec((1, 2 * D_FF), lambda i: (0, 0)),
                  _vmem_full()],
        out_specs=pl.BlockSpec((tm, D_MODEL), lambda i: (i, 0)),
        out_shape=jax.ShapeDtypeStruct((T, D_MODEL), F32),
        scratch_shapes=[pltpu.VMEM((tm + SUBLANES, 2 * D_FF), F32),
                        pltpu.VMEM((tm, D_FF), BF16)],
        compiler_params=_cparams(1),
        name="ffn",
    )(x2, g, wup, cw, cb, wdn)


def _pad_lanes(v, width=LANES):
    return jnp.pad(v.astype(F32), (0, width - v.shape[0])).reshape(1, width)


def _pack_w_in(w):
    zeros = lambda n: jnp.zeros((D_MODEL, n), BF16)
    seg = lambda a, n: w[:, a:a + n].astype(BF16)
    o_fox = MLA_IN
    o_ssm = MLA_IN + FOX_IN
    o_gate = o_ssm + SSM_IN
    kr0 = MLA_Q_RANK + MLA_KV_RANK
    w_mla = jnp.concatenate([seg(0, kr0), zeros(MLA_NOPE), seg(kr0, MLA_ROPE), zeros(LANES - MLA_QK_DIM)],
                            axis=1)
    w_small = jnp.concatenate([seg(o_fox + 3 * FOX_WIDTH, FOX_HEADS), zeros(LANES - FOX_HEADS),
                               seg(o_ssm + SSM_INNER + SSM_XBC, SSM_HEADS), zeros(LANES - SSM_HEADS)], axis=1)
    assert w_mla.shape[1] == MLA_SEG and w_small.shape[1] == SMALL_W
    return [w_mla, seg(o_fox, 3 * FOX_WIDTH), w_small, seg(o_ssm, SSM_INNER), seg(o_ssm + SSM_INNER, SSM_XBC),
            seg(o_gate, 3 * D_MODEL)]


def _one_hot_row(lanes, value=1.0):
    row = np.zeros((1, LANES), np.float32)
    row[0, lanes] = value
    return row


def _fox_placement():
    rows = np.arange(3 * FOX_HEADS)
    piece = rows // FOX_HEADS
    head = rows % FOX_HEADS
    pq = np.zeros((LANES, FOX_HEADS * HEAD_PAD), np.float32)
    pk = np.zeros((LANES, FOX_HEADS * HEAD_PAD), np.float32)
    pq[rows, head * HEAD_PAD + AUG0 + piece] = 1.0
    pk[rows, head * HEAD_PAD + AUG0 + 3 + piece] = -1.0
    oneq = _one_hot_row(slice(AUG0 + 3, AUG0 + 6))
    onek = _one_hot_row(slice(AUG0, AUG0 + 3))
    return jnp.asarray(pq, BF16), jnp.asarray(pk, BF16), jnp.asarray(oneq), jnp.asarray(onek)


def _shift_rows(bound, lane):
    return -bound * jnp.asarray(_one_hot_row(lane)), jnp.asarray(_one_hot_row(lane))


def _head_expand():
    h = np.arange(3 * SSM_HEADS)
    e = np.zeros((LANES, SSM_INNER), np.float32)
    cols = (h % SSM_HEADS)[:, None] * SSM_HEAD_DIM + np.arange(SSM_HEAD_DIM)[None, :]
    e[h[:, None], cols] = 1.0
    return jnp.asarray(e, BF16)


def kernel(x, positions, norm_mix_g, w_in, b_gate, mla_q_norm_g, mla_w_uq, mla_kv_norm_g, mla_w_ukv,
           mla_q_gain, mla_k_gain, fox_q_gain, fox_k_gain, fox_b_f, ssm_conv_w, ssm_conv_b, ssm_dt_bias,
           ssm_A_log, ssm_D, ssm_norm_g, w_br_mla, w_br_fox, w_br_ssm, w_out, norm_ffn_g, ffn_w_up,
           ffn_conv_w, ffn_conv_b, ffn_w_down):
    B, S, D = x.shape
    T = B * S
    depth = w_in.shape[0]
    tile = min(256, S)
    td = min(DENSE_TILE, S)
    ta = min(ATTN_TILE, S)

    pos = positions.reshape(T, 1).astype(F32)
    cos, sa, sb = _rope_tables(pos, tile)
    pq, pk, oneq, onek = _fox_placement()
    e01 = _head_expand()
    row = lambda v: v.reshape(1, -1).astype(F32)

    x2 = x.reshape(T, D).astype(F32)
    for l in range(depth):
        wq = jnp.pad(mla_w_uq[l].reshape(MLA_Q_RANK, MLA_HEADS, MLA_QK_DIM),
                     ((0, 0), (0, 0), (0, HEAD_PAD - MLA_QK_DIM))).reshape(MLA_Q_RANK, -1).astype(BF16)
        wkv = mla_w_ukv[l].reshape(MLA_KV_RANK, MLA_HEADS, MLA_NOPE + MLA_V)
        wk = jnp.pad(wkv[:, :, :MLA_NOPE], ((0, 0), (0, 0), (0, HEAD_PAD - MLA_NOPE))
                     ).reshape(MLA_KV_RANK, -1).astype(BF16)
        wv = wkv[:, :, MLA_NOPE:].reshape(MLA_KV_RANK, -1).astype(BF16)
        bound, flag_a = _score_bound(mla_q_gain[l], mla_k_gain[l], MLA_QK_DIM)
        qfix, kfix = _shift_rows(bound, MLA_QK_DIM)
        mla_in = [cos, sa, sb, row(mla_q_norm_g[l]), wq, row(mla_kv_norm_g[l]), wk, wv,
                  _pad_lanes(mla_q_gain[l]), _pad_lanes(mla_k_gain[l]), qfix, kfix]
        bound, flag_b = _score_bound(fox_q_gain[l], fox_k_gain[l], FOX_HEAD_DIM)
        qfix, kfix = _shift_rows(bound, AUG0 + 6)
        fox_in = [_pad_lanes(fox_b_f[l]), _pad_lanes(fox_q_gain[l]), _pad_lanes(fox_k_gain[l]), pq, pk,
                  oneq + qfix, onek + kfix]
        q, k, vt, qf, kf, vtf, u_dt, u_z, u_xbc, u_gate = _front(
            x2, row(norm_mix_g[l]), _pack_w_in(w_in[l]), mla_in, fox_in, B, S, tile, ta)
        y_a = _attention(flag_a, q, k, vt, ta, True).reshape(T, -1)
        y_b = _attention(flag_b, qf, kf, vtf, ta, False).reshape(T, -1)

        dx = jnp.repeat(ssm_D[l].astype(F32), SSM_HEAD_DIM).reshape(1, -1)
        y_c = _ssm(u_z, u_xbc, u_dt, ssm_conv_w[l].astype(F32), row(ssm_conv_b[l]),
                   _pad_lanes(ssm_dt_bias[l]), _pad_lanes(ssm_A_log[l]), dx, row(ssm_norm_g[l]), e01,
                   B, S, tile)

        x2 = _merge(x2, y_a, y_b, y_c, u_gate, row(b_gate[l]), w_br_mla[l].astype(BF16),
                    w_br_fox[l].astype(BF16), w_br_ssm[l].astype(BF16), w_out[l].astype(BF16), td)
        x2 = _ffn(x2, row(norm_ffn_g[l]), ffn_w_up[l].astype(BF16), ffn_conv_w[l].astype(F32),
                  row(ffn_conv_b[l]), ffn_w_down[l].astype(BF16), S, td)
    return x2.reshape(B, S, D).astype(x.dtype)
```

```python
import functools

import numpy as np
import jax
import jax.numpy as jnp
from jax import lax
from jax.experimental import pallas as pl
from jax.experimental.pallas import tpu as pltpu

F32 = jnp.float32
BF16 = jnp.bfloat16

D_MODEL = 1024
NORM_EPS = 1e-6
CHUNK = 64

MLA_HEADS = 8
MLA_Q_RANK = 384
MLA_KV_RANK = 256
MLA_NOPE = 64
MLA_ROPE = 32
MLA_QK_DIM = MLA_NOPE + MLA_ROPE
MLA_V = 64
ROPE_THETA = 10000.0

FOX_HEADS = 8
FOX_HEAD_DIM = 64
FOX_WIDTH = FOX_HEADS * FOX_HEAD_DIM

SSM_HEADS = 16
SSM_HEAD_DIM = 64
SSM_INNER = SSM_HEADS * SSM_HEAD_DIM
SSM_GROUPS = 2
SSM_STATE = 128
SSM_CONV = 4
SSM_XBC = SSM_INNER + 2 * SSM_GROUPS * SSM_STATE
SSM_GROUP_WIDTH = SSM_INNER // SSM_GROUPS

D_FF = 2816
FFN_CONV = 3

MLA_IN = MLA_Q_RANK + MLA_KV_RANK + MLA_ROPE
FOX_IN = 3 * FOX_WIDTH + FOX_HEADS
SSM_IN = SSM_INNER + SSM_XBC + SSM_HEADS

LANES = 128
SUBLANES = 8
HEAD_PAD = LANES
LOG2E = 1.4426950408889634
NEG_BIG = -1e30
VMEM_LIMIT = 52 * 1024 * 1024

MLA_SEG = MLA_Q_RANK + MLA_KV_RANK + LANES
SMALL_W = 2 * LANES

AUG0 = FOX_HEAD_DIM


def _cparams(n_grid):
    return pltpu.CompilerParams(dimension_semantics=("arbitrary",) * n_grid,
                                vmem_limit_bytes=VMEM_LIMIT)


def _vmem_full():
    return pl.BlockSpec(memory_space=pltpu.VMEM)


def _dot(a, b):
    return jnp.dot(a, b, preferred_element_type=F32)


def _dot_nt(a, b):
    return lax.dot_general(a, b, (((1,), (1,)), ((), ())), preferred_element_type=F32)


def _split3(x):
    hi = x.astype(BF16)
    r = x - hi.astype(F32)
    mid = r.astype(BF16)
    lo = (r - mid.astype(F32)).astype(BF16)
    return hi, mid, lo


def _dot3_left(m01, x):
    hi, mid, lo = _split3(x)
    return _dot(m01, hi) + _dot(m01, mid) + _dot(m01, lo)


def _dot3_right(x, m01):
    hi, mid, lo = _split3(x)
    return _dot(hi, m01) + _dot(mid, m01) + _dot(lo, m01)


def _sigmoid(x):
    return 1.0 / (1.0 + jnp.exp(-x))


def _silu(x):
    return x * _sigmoid(x)


def _softplus(x):
    return jnp.maximum(x, 0.0) + jnp.log(1.0 + jnp.exp(-jnp.abs(x)))


def _tril_bf16(n):
    r = lax.broadcasted_iota(jnp.int32, (n, n), 0)
    c = lax.broadcasted_iota(jnp.int32, (n, n), 1)
    return jnp.where(r >= c, 1.0, 0.0).astype(BF16)


def _rope_kernel(pos_ref, inv_ref, cos_ref, sa_ref, sb_ref):
    ang = pos_ref[...] * inv_ref[...]
    lane = lax.broadcasted_iota(jnp.int32, ang.shape, 1)
    c = jnp.cos(ang)
    s = jnp.sin(ang)
    half = MLA_ROPE // 2
    first = (lane >= MLA_NOPE) & (lane < MLA_NOPE + half)
    second = (lane >= MLA_NOPE + half) & (lane < MLA_QK_DIM)
    cos_ref[...] = jnp.where(lane < MLA_NOPE, 1.0, jnp.where(first | second, c, 0.0))
    sa_ref[...] = jnp.where(first, -s, 0.0)
    sb_ref[...] = jnp.where(second, s, 0.0)


def _rope_tables(pos, ts):
    T = pos.shape[0]
    half = MLA_ROPE // 2
    inv = 1.0 / (ROPE_THETA ** (jnp.arange(0, MLA_ROPE, 2, dtype=F32) / MLA_ROPE))
    inv_lane = jnp.concatenate([jnp.zeros((MLA_NOPE,), F32), inv, inv,
                                jnp.zeros((LANES - MLA_QK_DIM,), F32)]).reshape(1, LANES)
    tab = jax.ShapeDtypeStruct((T, LANES), F32)
    spec = pl.BlockSpec((ts, LANES), lambda i: (i, 0))
    return pl.pallas_call(
        _rope_kernel,
        grid=(T // ts,),
        in_specs=[pl.BlockSpec((ts, 1), lambda i: (i, 0)), pl.BlockSpec((1, LANES), lambda i: (0, 0))],
        out_specs=[spec, spec, spec],
        out_shape=[tab, tab, tab],
        compiler_params=_cparams(1),
        name="rope_tables",
    )(pos, inv_lane)


def _rope(x, cos, sa, sb):
    half = MLA_ROPE // 2
    return x * cos + pltpu.roll(x, LANES - half, axis=1) * sa + pltpu.roll(x, half, axis=1) * sb


IN_CH = 512
DENSE_TILE = 512


def _mla_prep(u, cos_ref, sa_ref, sb_ref, gq_ref, wq_ref, gkv_ref, wk_ref, wv_ref,
              qg_ref, kg_ref, qfix_ref, kfix_ref, q_out, k_out, vt_out):
    cos = cos_ref[...]
    sa = sa_ref[...]
    sb = sb_ref[...]
    cq = u[:, :MLA_Q_RANK]
    ckv = u[:, MLA_Q_RANK:MLA_Q_RANK + MLA_KV_RANK]
    kr = u[:, MLA_Q_RANK + MLA_KV_RANK:]
    cqn = (cq * lax.rsqrt(jnp.mean(cq * cq, axis=-1, keepdims=True) + NORM_EPS) * gq_ref[...]).astype(BF16)
    ckvn = (ckv * lax.rsqrt(jnp.mean(ckv * ckv, axis=-1, keepdims=True) + NORM_EPS) * gkv_ref[...]).astype(BF16)
    q_all = _dot(cqn, wq_ref[...])
    k_all = _dot(ckvn, wk_ref[...])
    v_all = _dot(ckvn, wv_ref[...])
    kr_ss = jnp.sum(kr * kr, axis=-1, keepdims=True)
    kr_rot = _rope(kr * kg_ref[...], cos, sa, sb)
    q_scale = (MLA_QK_DIM ** -0.5) * LOG2E
    for h in range(MLA_HEADS):
        qh = q_all[:, h * HEAD_PAD:(h + 1) * HEAD_PAD]
        ms = jnp.sum(qh * qh, axis=-1, keepdims=True) * (1.0 / MLA_QK_DIM)
        qh = qh * lax.rsqrt(ms + NORM_EPS) * qg_ref[...]
        q_out[0, h] = (_rope(qh, cos, sa, sb) * q_scale + qfix_ref[...]).astype(q_out.dtype)
        kn = k_all[:, h * HEAD_PAD:(h + 1) * HEAD_PAD]
        ms = (jnp.sum(kn * kn, axis=-1, keepdims=True) + kr_ss) * (1.0 / MLA_QK_DIM)
        kh = (kn * kg_ref[...] + kr_rot) * lax.rsqrt(ms + NORM_EPS)
        k_out[0, h] = (kh + kfix_ref[...]).astype(k_out.dtype)
    vt_out[0, :, 0] = _vt_tile(v_all, MLA_HEADS)


def _fox_prep(u, f_raw, bf_ref, qg_ref, kg_ref, pq_ref, pk_ref, oneq_ref, onek_ref,
              q_out, k_out, vt_out, carry):
    ts = u.shape[0]
    lane = lax.broadcasted_iota(jnp.int32, (ts, LANES), 1)
    f = f_raw + bf_ref[...]
    log_f = jnp.minimum(f, 0.0) - jnp.log(1.0 + jnp.exp(-jnp.abs(f)))
    log_f = jnp.where(lane < FOX_HEADS, log_f * LOG2E, 0.0)
    cum = carry[...] + _dot3_left(_tril_bf16(ts), log_f)
    carry[...] = cum[ts - 1:ts, :]
    hi, mid, lo = _split3(cum)
    packed = (hi.astype(F32) + pltpu.roll(mid.astype(F32), FOX_HEADS, axis=1)
              + pltpu.roll(lo.astype(F32), 2 * FOX_HEADS, axis=1)).astype(BF16)
    aug_q = _dot(packed, pq_ref[...])
    aug_k = _dot(packed, pk_ref[...])

    q_scale = (FOX_HEAD_DIM ** -0.5) * LOG2E
    head_lanes = lane < FOX_HEAD_DIM
    for h in range(FOX_HEADS):
        t0 = (h // 2) * LANES
        for src, gain, aug, ones, scale, out in (
                (0, qg_ref, aug_q, oneq_ref, q_scale, q_out),
                (FOX_WIDTH, kg_ref, aug_k, onek_ref, 1.0, k_out)):
            blk = u[:, src + t0:src + t0 + LANES]
            if h % 2 == 1:
                blk = pltpu.roll(blk, FOX_HEAD_DIM, axis=1)
            blk = jnp.where(head_lanes, blk, 0.0)
            ms = jnp.sum(blk * blk, axis=-1, keepdims=True) * (1.0 / FOX_HEAD_DIM)
            blk = blk * lax.rsqrt(ms + NORM_EPS) * (gain[...] * scale)
            blk = blk + aug[:, h * HEAD_PAD:(h + 1) * HEAD_PAD] + ones[...]
            out[0, h] = blk.astype(out.dtype)
    vt_out[0, :, 0] = _vt_tile(u[:, 2 * FOX_WIDTH:], FOX_HEADS)


N_SEG = 6
N_MLA = 12
N_FOX = 7


def _front_kernel(x_ref, g_ref, *refs):
    w_mla, w_fox, w_small, w_z, w_xbc, w_gate = refs[:N_SEG]
    mla_in = refs[N_SEG:N_SEG + N_MLA]
    fox_in = refs[N_SEG + N_MLA:N_SEG + N_MLA + N_FOX]
    cw_ref, cb_ref = refs[N_SEG + N_MLA + N_FOX:N_SEG + N_MLA + N_FOX + 2]
    outs = refs[N_SEG + N_MLA + N_FOX + 2:]
    q_out, k_out, vt_out, qf_out, kf_out, vtf_out, dt_out, z_out, xbc_out, gate_out, carry, xpad = outs
    ts = x_ref.shape[0]
    pad = SUBLANES

    @pl.when(pl.program_id(1) == 0)
    def _():
        carry[...] = jnp.zeros_like(carry)
        xpad[0:pad, :] = jnp.zeros((pad, SSM_XBC), F32)

    x = x_ref[...]
    hn = (x * lax.rsqrt(jnp.mean(x * x, axis=-1, keepdims=True) + NORM_EPS) * g_ref[...]).astype(BF16)
    u_mla = _dot(hn, w_mla[...])
    u_fox = _dot(hn, w_fox[...])
    u_small = _dot(hn, w_small[...])
    dt_out[...] = u_small[:, LANES:]
    _mla_prep(u_mla, *mla_in, q_out, k_out, vt_out)
    _fox_prep(u_fox, u_small[:, :LANES], *fox_in, qf_out, kf_out, vtf_out, carry)
    for cc in range(0, SSM_INNER, IN_CH):
        cs = slice(cc, cc + IN_CH)
        z_out[:, cs] = _silu(_dot(hn, w_z[:, cs])).astype(z_out.dtype)
    for cc in range(0, SSM_XBC, IN_CH):
        cs = slice(cc, cc + IN_CH)
        xpad[pad:pad + ts, cs] = _dot(hn, w_xbc[:, cs])
        xp = xpad[:, cs]
        conv = cb_ref[:, cs] + cw_ref[SSM_CONV - 1:SSM_CONV, cs] * xp[pad:, :]
        for back in range(1, SSM_CONV):
            kk = SSM_CONV - 1 - back
            conv = conv + cw_ref[kk:kk + 1, cs] * pltpu.roll(xp, back, axis=0)[pad:, :]
        xpad[0:pad, cs] = xpad[ts:ts + pad, cs]
        xbc_out[:, cs] = _silu(conv).astype(xbc_out.dtype)
    for cc in range(0, 3 * D_MODEL, IN_CH):
        cs = slice(cc, cc + IN_CH)
        gate_out[:, cs] = _dot(hn, w_gate[:, cs]).astype(gate_out.dtype)


def _front(x2, g, weights, mla_in, fox_in, conv_w, conv_b, B, S, ts, tk):
    nb = S // ts
    T = B * S
    H = MLA_HEADS
    per_tile = tk // ts
    assert len(weights) == N_SEG and len(mla_in) == N_MLA and len(fox_in) == N_FOX and tk % ts == 0
    tok = lambda b, i: (b * nb + i, 0)
    const = lambda b, i: (0, 0)
    row = lambda w: pl.BlockSpec((1, w), const)
    tab = pl.BlockSpec((ts, LANES), tok)
    head_major = pl.BlockSpec((1, H, ts, HEAD_PAD), lambda b, i: (b, 0, i, 0))
    vt_spec = pl.BlockSpec((1, H, 1, V_ROWS, ts), lambda b, i: (b, 0, i // per_tile, 0, i % per_tile))
    qk_shape = jax.ShapeDtypeStruct((B, H, S, HEAD_PAD), BF16)
    vt_shape = jax.ShapeDtypeStruct((B, H, S // tk, V_ROWS, tk), BF16)
    wide = lambda n: (pl.BlockSpec((ts, n), tok), jax.ShapeDtypeStruct((T, n), BF16))
    z_spec, z_shape = wide(SSM_INNER)
    xbc_spec, xbc_shape = wide(SSM_XBC)
    gate_spec, gate_shape = wide(3 * D_MODEL)
    return pl.pallas_call(
        _front_kernel,
        grid=(B, nb),
        in_specs=[pl.BlockSpec((ts, D_MODEL), tok), row(D_MODEL)] + [_vmem_full()] * N_SEG
                 + [tab, tab, tab, row(MLA_Q_RANK), _vmem_full(), row(MLA_KV_RANK), _vmem_full(), _vmem_full(),
                    row(LANES), row(LANES), row(LANES), row(LANES)]
                 + [row(LANES), row(LANES), row(LANES), _vmem_full(), _vmem_full(), row(LANES), row(LANES)]
                 + [pl.BlockSpec((SSM_CONV, SSM_XBC), const), row(SSM_XBC)],
        out_specs=[head_major, head_major, vt_spec, head_major, head_major, vt_spec,
                   tab, z_spec, xbc_spec, gate_spec],
        out_shape=[qk_shape, qk_shape, vt_shape, qk_shape, qk_shape, vt_shape,
                   jax.ShapeDtypeStruct((T, LANES), F32), z_shape, xbc_shape, gate_shape],
        scratch_shapes=[pltpu.VMEM((1, LANES), F32), pltpu.VMEM((ts + SUBLANES, SSM_XBC), F32)],
        compiler_params=_cparams(2),
        name="front",
    )(x2, g, *weights, *mla_in, *fox_in, conv_w, conv_b)


ATTN_TILE = 512
V_ROWS = 64


def _vt_tile(v_all, heads):
    ts = v_all.shape[0]
    return v_all.T.reshape(heads, V_ROWS, ts).astype(BF16)


HEADS_PER_STEP = 4
BOUND_SLACK = 1.02
BOUND_STEP = 0.25
MAX_SAFE_BOUND = 60.0


def _score_bound(q_gain, k_gain, dim):
    bound = dim * jnp.max(jnp.abs(q_gain)) * jnp.max(jnp.abs(k_gain)) * (dim ** -0.5) * LOG2E * BOUND_SLACK
    bound = jnp.ceil(bound.astype(F32) / BOUND_STEP) * BOUND_STEP
    safe = bound < MAX_SAFE_BOUND
    return jnp.where(safe, bound, 0.0), safe.astype(jnp.int32).reshape(1)


def _attn_kernel(flag_ref, q_ref, k_ref, vt_ref, o_ref, *, tq, chunk_causal):
    i = pl.program_id(2)
    hs = q_ref.shape[1]
    dv = vt_ref.shape[3]

    half = tq // 2
    q_t = [q_ref[0, hh].astype(F32).T.astype(BF16) for hh in range(hs)]

    def causal_mask(shape):
        r = lax.broadcasted_iota(jnp.int32, shape, 0)
        c = lax.broadcasted_iota(jnp.int32, shape, 1)
        shift = CHUNK.bit_length() - 1
        return ((r >> shift) <= (c >> shift)) if chunk_causal else (r <= c)

    def column_sums(p):
        return p.reshape(p.shape[0] // SUBLANES, SUBLANES, p.shape[1]).sum(axis=0)

    def pipeline(items, state):
        state = list(state)
        s_next = items[0][1]()
        for n, (hh, _, consume) in enumerate(items):
            s = s_next
            if n + 1 < len(items):
                s_next = items[n + 1][1]()
            state[hh] = consume(state[hh], s)
        return tuple(state)

    def full_items(js, update):
        items = []
        for j in js:
            for hh in range(hs):
                def scores(j=j, hh=hh):
                    kj = k_ref[0, hh, pl.ds(pl.multiple_of(j * tq, tq), tq), :]
                    return _dot(kj, q_t[hh])
                items.append((hh, scores, lambda st, s, j=j, hh=hh: update(st, s, vt_ref[0, hh, j])))
        return items

    def sweep(update, init, diagonal, finish):
        body = lambda n: (lambda jj, st: pipeline(full_items([n * jj + t for t in range(n)], update), st))
        quads = lax.shift_right_logical(i, 2)
        state = lax.fori_loop(0, quads, body(4), init)
        done = quads * 4
        state = lax.fori_loop(0, lax.shift_right_logical(i - done, 1),
                              lambda _, st: pipeline(full_items([done, done + 1], update), st), state)
        state = lax.fori_loop(0, i & 1, lambda _, st: pipeline(full_items([i - 1], update), st), state)
        state = diagonal(state)
        o = jnp.concatenate([finish(st) for st in state], axis=0)
        o_ref[0] = o.T.astype(o_ref.dtype)

    @pl.when(flag_ref[0] == 1)
    def _():
        def update(st, s, vt):
            l8, acc = st
            p = jnp.exp2(s)
            return l8 + column_sums(p), acc + _dot(vt, p.astype(BF16))

        def diagonal(state):
            ok_a = causal_mask((half, tq))
            ok_b = causal_mask((half, half))
            items = []
            for hh in range(hs):
                def scores_a(hh=hh):
                    ka = k_ref[0, hh, pl.ds(pl.multiple_of(i * tq, tq), half), :]
                    return _dot(ka, q_t[hh])
                def consume_a(st, s, hh=hh):
                    return update(st, jnp.where(ok_a, s, NEG_BIG), vt_ref[0, hh, i, :, :half])
                def scores_b(hh=hh):
                    kb = k_ref[0, hh, pl.ds(pl.multiple_of(i * tq + half, half), half), :]
                    return _dot(kb, q_t[hh][:, half:])
                def consume_b(st, s, hh=hh):
                    l8, acc = st
                    p = jnp.exp2(jnp.where(ok_b, s, NEG_BIG))
                    l8 = jnp.concatenate([l8[:, :half], l8[:, half:] + column_sums(p)], axis=1)
                    pv = _dot(vt_ref[0, hh, i, :, half:], p.astype(BF16))
                    return l8, jnp.concatenate([acc[:, :half], acc[:, half:] + pv], axis=1)
                items += [(hh, scores_a, consume_a), (hh, scores_b, consume_b)]
            return pipeline(items, state)

        init = tuple((jnp.zeros((SUBLANES, tq), F32), jnp.zeros((dv, tq), F32)) for _ in range(hs))
        sweep(update, init, diagonal, lambda st: st[1] * (1.0 / jnp.sum(st[0], axis=0, keepdims=True)))

    @pl.when(flag_ref[0] != 1)
    def _():
        def update(st, s, vt):
            m, l8, acc = st
            m_new = jnp.maximum(m, jnp.max(s, axis=0, keepdims=True))
            alpha = jnp.exp2(m - m_new)
            p = jnp.exp2(s - m_new)
            return m_new, alpha * l8 + column_sums(p), alpha * acc + _dot(vt, p.astype(BF16))

        def diagonal(state):
            ok = causal_mask((tq, tq))
            masked = lambda st, s, vt: update(st, jnp.where(ok, s, NEG_BIG), vt)
            return pipeline(full_items([i], masked), state)

        init = tuple((jnp.full((1, tq), NEG_BIG, F32), jnp.zeros((SUBLANES, tq), F32),
                      jnp.zeros((dv, tq), F32)) for _ in range(hs))
        sweep(update, init, diagonal, lambda st: st[2] * (1.0 / jnp.sum(st[1], axis=0, keepdims=True)))


def _attention(flag, q, k, vt, tq, chunk_causal):
    B, H, S, _ = q.shape
    dv = vt.shape[3]
    hs = HEADS_PER_STEP
    assert vt.shape[4] == tq and (hs * dv) % LANES == 0 and H % hs == 0
    nq = S // tq
    return pl.pallas_call(
        functools.partial(_attn_kernel, tq=tq, chunk_causal=chunk_causal),
        grid=(B, H // hs, nq),
        in_specs=[pl.BlockSpec(memory_space=pltpu.SMEM),
                  pl.BlockSpec((1, hs, tq, HEAD_PAD), lambda b, h, i: (b, h, i, 0)),
                  pl.BlockSpec((1, hs, S, HEAD_PAD), lambda b, h, i: (b, h, 0, 0)),
                  pl.BlockSpec((1, hs, nq, dv, tq), lambda b, h, i: (b, h, 0, 0, 0))],
        out_specs=pl.BlockSpec((1, tq, hs * dv), lambda b, h, i: (b, i, h)),
        out_shape=jax.ShapeDtypeStruct((B, S, H * dv), BF16),
        compiler_params=_cparams(3),
        name="attn_mla" if chunk_causal else "attn_fox",
    )(flag, q, k, vt)


def _ssm_kernel(zs_ref, xc_ref, dt_ref, dtb_ref, alog_ref, dx_ref, ng_ref, e_ref, o_ref, hst, *, lc):
    @pl.when(pl.program_id(1) == 0)
    def _():
        hst[...] = jnp.zeros_like(hst)

    xc = xc_ref[...].astype(F32)
    xs = xc[:, :SSM_INNER]
    bm = xc[:, SSM_INNER:SSM_INNER + SSM_GROUPS * SSM_STATE]
    cm = xc[:, SSM_INNER + SSM_GROUPS * SSM_STATE:]

    lane = lax.broadcasted_iota(jnp.int32, (lc, LANES), 1)
    dt = jnp.where(lane < SSM_HEADS, _softplus(dt_ref[...] + dtb_ref[...]), 0.0)
    a_neg = -jnp.exp(alog_ref[...]) * LOG2E
    a_cum = _dot3_left(_tril_bf16(lc), dt * a_neg)
    head_lanes = lane < SSM_HEADS
    ea = jnp.where(head_lanes, jnp.exp2(a_cum), 0.0)
    dte = jnp.where(head_lanes, jnp.exp2(a_cum[lc - 1:lc, :] - a_cum), 0.0)
    e01 = e_ref[...]

    def expand(v):
        hi, mid, lo = _split3(v)
        packed = (hi.astype(F32) + pltpu.roll(mid.astype(F32), SSM_HEADS, axis=1)
                  + pltpu.roll(lo.astype(F32), 2 * SSM_HEADS, axis=1)).astype(BF16)
        return _dot(packed, e01)

    dt_x = expand(dt)
    ea_x = expand(ea)
    dte_x = expand(dte)
    xdt = xs * dt_x
    a_t = a_cum.T

    row = lax.broadcasted_iota(jnp.int32, (lc, lc), 0)
    col = lax.broadcasted_iota(jnp.int32, (lc, lc), 1)
    causal = row >= col
    lane_first = lane < SSM_HEAD_DIM
    heads_per_group = SSM_HEADS // SSM_GROUPS
    y_blocks = []
    for g in range(SSM_GROUPS):
        bg = bm[:, g * SSM_STATE:(g + 1) * SSM_STATE]
        cg = cm[:, g * SSM_STATE:(g + 1) * SSM_STATE].astype(BF16)
        cb = _dot_nt(cg, bg.astype(BF16))
        gs = slice(g * SSM_GROUP_WIDTH, (g + 1) * SSM_GROUP_WIDTH)
        h_prev = hst[g]
        y_off = _dot(cg, h_prev.astype(BF16)) * ea_x[:, gs]
        xw = (xdt[:, gs] * dte_x[:, gs]).astype(BF16)
        s_new = _dot(bg.T.astype(BF16), xw)
        hst[g] = h_prev * ea_x[lc - 1:lc, gs] + s_new
        for hp in range(heads_per_group // 2):
            t0 = g * SSM_GROUP_WIDTH + hp * LANES
            x_pair = xdt[:, t0:t0 + LANES]
            acc = None
            for e in range(2):
                h = g * heads_per_group + hp * 2 + e
                seg = a_cum[:, h:h + 1] - a_t[h:h + 1, :]
                decay = jnp.exp2(jnp.where(causal, seg, NEG_BIG))
                gmat = (cb * decay).astype(BF16)
                xh = jnp.where(lane_first if e == 0 else ~lane_first, x_pair, 0.0).astype(BF16)
                part = _dot(gmat, xh)
                acc = part if acc is None else acc + part
            y_blocks.append(acc + y_off[:, hp * LANES:(hp + 1) * LANES])
    y = jnp.concatenate(y_blocks, axis=1) + xs * dx_ref[...]
    yg = y * zs_ref[...].astype(F32)
    o_ref[...] = (yg * lax.rsqrt(jnp.mean(yg * yg, axis=-1, keepdims=True) + NORM_EPS)
                  * ng_ref[...]).astype(o_ref.dtype)


def _ssm(zs, xc, u_dt, dtb, alog, dx, ng, e01, B, S, lc):
    nb = S // lc
    tok = lambda b, i: (b * nb + i, 0)
    const = lambda b, i: (0, 0)
    return pl.pallas_call(
        functools.partial(_ssm_kernel, lc=lc),
        grid=(B, nb),
        in_specs=[pl.BlockSpec((lc, SSM_INNER), tok),
                  pl.BlockSpec((lc, SSM_XBC), tok),
                  pl.BlockSpec((lc, LANES), tok),
                  pl.BlockSpec((1, LANES), const), pl.BlockSpec((1, LANES), const),
                  pl.BlockSpec((1, SSM_INNER), const), pl.BlockSpec((1, SSM_INNER), const),
                  _vmem_full()],
        out_specs=pl.BlockSpec((lc, SSM_INNER), tok),
        out_shape=jax.ShapeDtypeStruct((B * S, SSM_INNER), BF16),
        scratch_shapes=[pltpu.VMEM((SSM_GROUPS, SSM_STATE, SSM_GROUP_WIDTH), F32)],
        compiler_params=_cparams(2),
        name="ssm",
    )(zs, xc, u_dt, dtb, alog, dx, ng, e01)


def _merge_kernel(x_ref, ya_ref, yb_ref, yc_ref, ug_ref, bg_ref, wa_ref, wb_ref, wc_ref, wo_ref, o_ref):
    ys = (_dot(ya_ref[...], wa_ref[...]), _dot(yb_ref[...], wb_ref[...]), _dot(yc_ref[...], wc_ref[...]))
    merged = None
    for n, y in enumerate(ys):
        cs = slice(n * D_MODEL, (n + 1) * D_MODEL)
        gate = _sigmoid(ug_ref[:, cs].astype(F32) + bg_ref[:, cs])
        merged = gate * y if merged is None else merged + gate * y
    o_ref[...] = x_ref[...] + _dot(merged.astype(BF16), wo_ref[...])


def _merge(x2, ya, yb, yc, ug, bg, wa, wb, wc, wo, tm):
    T = x2.shape[0]
    row = lambda w: pl.BlockSpec((tm, w), lambda i: (i, 0))
    return pl.pallas_call(
        _merge_kernel,
        grid=(T // tm,),
        in_specs=[row(D_MODEL), row(MLA_HEADS * MLA_V), row(FOX_WIDTH), row(SSM_INNER), row(3 * D_MODEL),
                  pl.BlockSpec((1, 3 * D_MODEL), lambda i: (0, 0)),
                  _vmem_full(), _vmem_full(), _vmem_full(), _vmem_full()],
        out_specs=row(D_MODEL),
        out_shape=jax.ShapeDtypeStruct((T, D_MODEL), F32),
        compiler_params=_cparams(1),
        name="merge",
    )(x2, ya, yb, yc, ug, bg, wa, wb, wc, wo)


FF_CH = 256


def _ffn_kernel(x_ref, g_ref, wup_ref, cw_ref, cb_ref, wdn_ref, o_ref, abuf, hbuf, *, tm, blocks_per_seq):
    pad = SUBLANES

    @pl.when(pl.program_id(0) % blocks_per_seq == 0)
    def _():
        abuf[0:pad, :] = jnp.zeros((pad, 2 * D_FF), F32)

    x = x_ref[...]
    hn = (x * lax.rsqrt(jnp.mean(x * x, axis=-1, keepdims=True) + NORM_EPS) * g_ref[...]).astype(BF16)
    for c0 in range(0, D_FF, FF_CH):
        halves = []
        for base in (c0, D_FF + c0):
            cs = slice(base, base + FF_CH)
            abuf[pad:pad + tm, cs] = _dot(hn, wup_ref[:, cs])
            ap = abuf[:, cs]
            conv = cb_ref[:, cs] + cw_ref[FFN_CONV - 1:FFN_CONV, cs] * ap[pad:, :]
            for back in range(1, FFN_CONV):
                kk = FFN_CONV - 1 - back
                conv = conv + cw_ref[kk:kk + 1, cs] * pltpu.roll(ap, back, axis=0)[pad:, :]
            abuf[0:pad, cs] = abuf[tm:tm + pad, cs]
            halves.append(conv)
        hbuf[:, c0:c0 + FF_CH] = (_silu(halves[0]) * halves[1]).astype(BF16)
    o_ref[...] = x + _dot(hbuf[...], wdn_ref[...])


def _ffn(x2, g, wup, cw, cb, wdn, S, tm):
    T = x2.shape[0]
    return pl.pallas_call(
        functools.partial(_ffn_kernel, tm=tm, blocks_per_seq=S // tm),
        grid=(T // tm,),
        in_specs=[pl.BlockSpec((tm, D_MODEL), lambda i: (i, 0)),
                  pl.BlockSpec((1, D_MODEL), lambda i: (0, 0)),
                  _vmem_full(),
                  pl.BlockSpec((FFN_CONV, 2 * D_FF), lambda i: (0, 0)),
                  pl.BlockSpec((1, 2 * D_FF), lambda i: (0, 0)),
                  _vmem_full()],
        out_specs=pl.BlockSpec((tm, D_MODEL), lambda i: (i, 0)),
        out_shape=jax.ShapeDtypeStruct((T, D_MODEL), F32),
        scratch_shapes=[pltpu.VMEM((tm + SUBLANES, 2 * D_FF), F32),
                        pltpu.VMEM((tm, D_FF), BF16)],
        compiler_params=_cparams(1),
        name="ffn",
    )(x2, g, wup, cw, cb, wdn)


def _pad_lanes(v, width=LANES):
    return jnp.pad(v.astype(F32), (0, width - v.shape[0])).reshape(1, width)


def _pack_w_in(w):
    zeros = lambda n: jnp.zeros((D_MODEL, n), BF16)
    seg = lambda a, n: w[:, a:a + n].astype(BF16)
    o_fox = MLA_IN
    o_ssm = MLA_IN + FOX_IN
    o_gate = o_ssm + SSM_IN
    kr0 = MLA_Q_RANK + MLA_KV_RANK
    w_mla = jnp.concatenate([seg(0, kr0), zeros(MLA_NOPE), seg(kr0, MLA_ROPE), zeros(LANES - MLA_QK_DIM)],
                            axis=1)
    w_small = jnp.concatenate([seg(o_fox + 3 * FOX_WIDTH, FOX_HEADS), zeros(LANES - FOX_HEADS),
                               seg(o_ssm + SSM_INNER + SSM_XBC, SSM_HEADS), zeros(LANES - SSM_HEADS)], axis=1)
    assert w_mla.shape[1] == MLA_SEG and w_small.shape[1] == SMALL_W
    return [w_mla, seg(o_fox, 3 * FOX_WIDTH), w_small, seg(o_ssm, SSM_INNER), seg(o_ssm + SSM_INNER, SSM_XBC),
            seg(o_gate, 3 * D_MODEL)]


def _one_hot_row(lanes, value=1.0):
    row = np.zeros((1, LANES), np.float32)
    row[0, lanes] = value
    return row


def _fox_placement():
    rows = np.arange(3 * FOX_HEADS)
    piece = rows // FOX_HEADS
    head = rows % FOX_HEADS
    pq = np.zeros((LANES, FOX_HEADS * HEAD_PAD), np.float32)
    pk = np.zeros((LANES, FOX_HEADS * HEAD_PAD), np.float32)
    pq[rows, head * HEAD_PAD + AUG0 + piece] = 1.0
    pk[rows, head * HEAD_PAD + AUG0 + 3 + piece] = -1.0
    oneq = _one_hot_row(slice(AUG0 + 3, AUG0 + 6))
    onek = _one_hot_row(slice(AUG0, AUG0 + 3))
    return jnp.asarray(pq, BF16), jnp.asarray(pk, BF16), jnp.asarray(oneq), jnp.asarray(onek)


def _shift_rows(bound, lane):
    return -bound * jnp.asarray(_one_hot_row(lane)), jnp.asarray(_one_hot_row(lane))


def _head_expand():
    h = np.arange(3 * SSM_HEADS)
    e = np.zeros((LANES, SSM_INNER), np.float32)
    cols = (h % SSM_HEADS)[:, None] * SSM_HEAD_DIM + np.arange(SSM_HEAD_DIM)[None, :]
    e[h[:, None], cols] = 1.0
    return jnp.asarray(e, BF16)


def kernel(x, positions, norm_mix_g, w_in, b_gate, mla_q_norm_g, mla_w_uq, mla_kv_norm_g, mla_w_ukv,
           mla_q_gain, mla_k_gain, fox_q_gain, fox_k_gain, fox_b_f, ssm_conv_w, ssm_conv_b, ssm_dt_bias,
           ssm_A_log, ssm_D, ssm_norm_g, w_br_mla, w_br_fox, w_br_ssm, w_out, norm_ffn_g, ffn_w_up,
           ffn_conv_w, ffn_conv_b, ffn_w_down):
    B, S, D = x.shape
    T = B * S
    depth = w_in.shape[0]
    tile = min(256, S)
    td = min(DENSE_TILE, S)
    ta = min(ATTN_TILE, S)

    pos = positions.reshape(T, 1).astype(F32)
    cos, sa, sb = _rope_tables(pos, tile)
    pq, pk, oneq, onek = _fox_placement()
    e01 = _head_expand()
    row = lambda v: v.reshape(1, -1).astype(F32)

    x2 = x.reshape(T, D).astype(F32)
    for l in range(depth):
        wq = jnp.pad(mla_w_uq[l].reshape(MLA_Q_RANK, MLA_HEADS, MLA_QK_DIM),
                     ((0, 0), (0, 0), (0, HEAD_PAD - MLA_QK_DIM))).reshape(MLA_Q_RANK, -1).astype(BF16)
        wkv = mla_w_ukv[l].reshape(MLA_KV_RANK, MLA_HEADS, MLA_NOPE + MLA_V)
        wk = jnp.pad(wkv[:, :, :MLA_NOPE], ((0, 0), (0, 0), (0, HEAD_PAD - MLA_NOPE))
                     ).reshape(MLA_KV_RANK, -1).astype(BF16)
        wv = wkv[:, :, MLA_NOPE:].reshape(MLA_KV_RANK, -1).astype(BF16)
        bound, flag_a = _score_bound(mla_q_gain[l], mla_k_gain[l], MLA_QK_DIM)
        qfix, kfix = _shift_rows(bound, MLA_QK_DIM)
        mla_in = [cos, sa, sb, row(mla_q_norm_g[l]), wq, row(mla_kv_norm_g[l]), wk, wv,
                  _pad_lanes(mla_q_gain[l]), _pad_lanes(mla_k_gain[l]), qfix, kfix]
        bound, flag_b = _score_bound(fox_q_gain[l], fox_k_gain[l], FOX_HEAD_DIM)
        qfix, kfix = _shift_rows(bound, AUG0 + 6)
        fox_in = [_pad_lanes(fox_b_f[l]), _pad_lanes(fox_q_gain[l]), _pad_lanes(fox_k_gain[l]), pq, pk,
                  oneq + qfix, onek + kfix]
        q, k, vt, qf, kf, vtf, u_dt, u_z, u_xbc, u_gate = _front(
            x2, row(norm_mix_g[l]), _pack_w_in(w_in[l]), mla_in, fox_in,
            ssm_conv_w[l].astype(F32), row(ssm_conv_b[l]), B, S, tile, ta)
        y_a = _attention(flag_a, q, k, vt, ta, True).reshape(T, -1)
        y_b = _attention(flag_b, qf, kf, vtf, ta, False).reshape(T, -1)

        dx = jnp.repeat(ssm_D[l].astype(F32), SSM_HEAD_DIM).reshape(1, -1)
        y_c = _ssm(u_z, u_xbc, u_dt, _pad_lanes(ssm_dt_bias[l]), _pad_lanes(ssm_A_log[l]), dx,
                   row(ssm_norm_g[l]), e01, B, S, tile)

        x2 = _merge(x2, y_a, y_b, y_c, u_gate, row(b_gate[l]), w_br_mla[l].astype(BF16),
                    w_br_fox[l].astype(BF16), w_br_ssm[l].astype(BF16), w_out[l].astype(BF16), td)
        x2 = _ffn(x2, row(norm_ffn_g[l]), ffn_w_up[l].astype(BF16), ffn_conv_w[l].astype(F32),
                  row(ffn_conv_b[l]), ffn_w_down[l].astype(BF16), S, td)
    return x2.reshape(B, S, D).astype(x.dtype)
```

```python
import functools

import numpy as np
import jax
import jax.numpy as jnp
from jax import lax
from jax.experimental import pallas as pl
from jax.experimental.pallas import tpu as pltpu

F32 = jnp.float32
BF16 = jnp.bfloat16

D_MODEL = 1024
NORM_EPS = 1e-6
CHUNK = 64

MLA_HEADS = 8
MLA_Q_RANK = 384
MLA_KV_RANK = 256
MLA_NOPE = 64
MLA_ROPE = 32
MLA_QK_DIM = MLA_NOPE + MLA_ROPE
MLA_V = 64
ROPE_THETA = 10000.0

FOX_HEADS = 8
FOX_HEAD_DIM = 64
FOX_WIDTH = FOX_HEADS * FOX_HEAD_DIM

SSM_HEADS = 16
SSM_HEAD_DIM = 64
SSM_INNER = SSM_HEADS * SSM_HEAD_DIM
SSM_GROUPS = 2
SSM_STATE = 128
SSM_CONV = 4
SSM_XBC = SSM_INNER + 2 * SSM_GROUPS * SSM_STATE
SSM_GROUP_WIDTH = SSM_INNER // SSM_GROUPS

D_FF = 2816
FFN_CONV = 3

MLA_IN = MLA_Q_RANK + MLA_KV_RANK + MLA_ROPE
FOX_IN = 3 * FOX_WIDTH + FOX_HEADS
SSM_IN = SSM_INNER + SSM_XBC + SSM_HEADS

LANES = 128
SUBLANES = 8
HEAD_PAD = LANES
LOG2E = 1.4426950408889634
NEG_BIG = -1e30
VMEM_LIMIT = 52 * 1024 * 1024

MLA_SEG = MLA_Q_RANK + MLA_KV_RANK + LANES
SMALL_W = 2 * LANES

AUG0 = FOX_HEAD_DIM


def _cparams(n_grid):
    return pltpu.CompilerParams(dimension_semantics=("arbitrary",) * n_grid,
                                vmem_limit_bytes=VMEM_LIMIT)


def _vmem_full():
    return pl.BlockSpec(memory_space=pltpu.VMEM)


def _dot(a, b):
    return jnp.dot(a, b, preferred_element_type=F32)


def _dot_nt(a, b):
    return lax.dot_general(a, b, (((1,), (1,)), ((), ())), preferred_element_type=F32)


def _split3(x):
    hi = x.astype(BF16)
    r = x - hi.astype(F32)
    mid = r.astype(BF16)
    lo = (r - mid.astype(F32)).astype(BF16)
    return hi, mid, lo


def _dot3_left(m01, x):
    hi, mid, lo = _split3(x)
    return _dot(m01, hi) + _dot(m01, mid) + _dot(m01, lo)


def _dot3_right(x, m01):
    hi, mid, lo = _split3(x)
    return _dot(hi, m01) + _dot(mid, m01) + _dot(lo, m01)


def _sigmoid(x):
    return 1.0 / (1.0 + jnp.exp(-x))


def _silu(x):
    return x * _sigmoid(x)


def _softplus(x):
    return jnp.maximum(x, 0.0) + jnp.log(1.0 + jnp.exp(-jnp.abs(x)))


def _tril_bf16(n):
    r = lax.broadcasted_iota(jnp.int32, (n, n), 0)
    c = lax.broadcasted_iota(jnp.int32, (n, n), 1)
    return jnp.where(r >= c, 1.0, 0.0).astype(BF16)


ROPE_TILE = 1024

def _rope_kernel(pos_ref, inv_ref, cos_ref, sa_ref, sb_ref):
    ang = pos_ref[...] * inv_ref[...]
    lane = lax.broadcasted_iota(jnp.int32, ang.shape, 1)
    c = jnp.cos(ang)
    s = jnp.sin(ang)
    half = MLA_ROPE // 2
    first = (lane >= MLA_NOPE) & (lane < MLA_NOPE + half)
    second = (lane >= MLA_NOPE + half) & (lane < MLA_QK_DIM)
    cos_ref[...] = jnp.where(lane < MLA_NOPE, 1.0, jnp.where(first | second, c, 0.0))
    sa_ref[...] = jnp.where(first, -s, 0.0)
    sb_ref[...] = jnp.where(second, s, 0.0)


def _rope_tables(pos, ts):
    T = pos.shape[0]
    half = MLA_ROPE // 2
    inv = 1.0 / (ROPE_THETA ** (jnp.arange(0, MLA_ROPE, 2, dtype=F32) / MLA_ROPE))
    inv_lane = jnp.concatenate([jnp.zeros((MLA_NOPE,), F32), inv, inv,
                                jnp.zeros((LANES - MLA_QK_DIM,), F32)]).reshape(1, LANES)
    tab = jax.ShapeDtypeStruct((T, LANES), F32)
    spec = pl.BlockSpec((ts, LANES), lambda i: (i, 0))
    return pl.pallas_call(
        _rope_kernel,
        grid=(T // ts,),
        in_specs=[pl.BlockSpec((ts, 1), lambda i: (i, 0)), pl.BlockSpec((1, LANES), lambda i: (0, 0))],
        out_specs=[spec, spec, spec],
        out_shape=[tab, tab, tab],
        compiler_params=_cparams(1),
        name="rope_tables",
    )(pos, inv_lane)


def _rope(x, cos, sa, sb):
    half = MLA_ROPE // 2
    return x * cos + pltpu.roll(x, LANES - half, axis=1) * sa + pltpu.roll(x, half, axis=1) * sb


IN_CH = 512
DENSE_TILE = 512


def _mla_prep(u, cos_ref, sa_ref, sb_ref, gq_ref, wq_ref, gkv_ref, wk_ref, wv_ref,
              qg_ref, kg_ref, qfix_ref, kfix_ref, q_out, k_out, vt_out):
    cos = cos_ref[...]
    sa = sa_ref[...]
    sb = sb_ref[...]
    cq = u[:, :MLA_Q_RANK]
    ckv = u[:, MLA_Q_RANK:MLA_Q_RANK + MLA_KV_RANK]
    kr = u[:, MLA_Q_RANK + MLA_KV_RANK:]
    cqn = (cq * lax.rsqrt(jnp.mean(cq * cq, axis=-1, keepdims=True) + NORM_EPS) * gq_ref[...]).astype(BF16)
    ckvn = (ckv * lax.rsqrt(jnp.mean(ckv * ckv, axis=-1, keepdims=True) + NORM_EPS) * gkv_ref[...]).astype(BF16)
    q_all = _dot(cqn, wq_ref[...])
    k_all = _dot(ckvn, wk_ref[...])
    v_all = _dot(ckvn, wv_ref[...])
    kr_ss = jnp.sum(kr * kr, axis=-1, keepdims=True)
    kr_rot = _rope(kr * kg_ref[...], cos, sa, sb)
    q_scale = (MLA_QK_DIM ** -0.5) * LOG2E
    for h in range(MLA_HEADS):
        qh = q_all[:, h * HEAD_PAD:(h + 1) * HEAD_PAD]
        ms = jnp.sum(qh * qh, axis=-1, keepdims=True) * (1.0 / MLA_QK_DIM)
        qh = qh * lax.rsqrt(ms + NORM_EPS) * qg_ref[...]
        q_out[0, h] = (_rope(qh, cos, sa, sb) * q_scale + qfix_ref[...]).astype(q_out.dtype)
        kn = k_all[:, h * HEAD_PAD:(h + 1) * HEAD_PAD]
        ms = (jnp.sum(kn * kn, axis=-1, keepdims=True) + kr_ss) * (1.0 / MLA_QK_DIM)
        kh = (kn * kg_ref[...] + kr_rot) * lax.rsqrt(ms + NORM_EPS)
        k_out[0, h] = (kh + kfix_ref[...]).astype(k_out.dtype)
    vt_out[0, :, 0] = _vt_tile(v_all, MLA_HEADS)


def _fox_prep(u, f_raw, bf_ref, qg_ref, kg_ref, pq_ref, pk_ref, oneq_ref, onek_ref,
              q_out, k_out, vt_out, carry):
    ts = u.shape[0]
    lane = lax.broadcasted_iota(jnp.int32, (ts, LANES), 1)
    f = f_raw + bf_ref[...]
    log_f = jnp.minimum(f, 0.0) - jnp.log(1.0 + jnp.exp(-jnp.abs(f)))
    log_f = jnp.where(lane < FOX_HEADS, log_f * LOG2E, 0.0)
    cum = carry[...] + _dot3_left(_tril_bf16(ts), log_f)
    carry[...] = cum[ts - 1:ts, :]
    hi, mid, lo = _split3(cum)
    packed = (hi.astype(F32) + pltpu.roll(mid.astype(F32), FOX_HEADS, axis=1)
              + pltpu.roll(lo.astype(F32), 2 * FOX_HEADS, axis=1)).astype(BF16)
    aug_q = _dot(packed, pq_ref[...])
    aug_k = _dot(packed, pk_ref[...])

    q_scale = (FOX_HEAD_DIM ** -0.5) * LOG2E
    head_lanes = lane < FOX_HEAD_DIM
    for h in range(FOX_HEADS):
        t0 = (h // 2) * LANES
        for src, gain, aug, ones, scale, out in (
                (0, qg_ref, aug_q, oneq_ref, q_scale, q_out),
                (FOX_WIDTH, kg_ref, aug_k, onek_ref, 1.0, k_out)):
            blk = u[:, src + t0:src + t0 + LANES]
            if h % 2 == 1:
                blk = pltpu.roll(blk, FOX_HEAD_DIM, axis=1)
            blk = jnp.where(head_lanes, blk, 0.0)
            ms = jnp.sum(blk * blk, axis=-1, keepdims=True) * (1.0 / FOX_HEAD_DIM)
            blk = blk * lax.rsqrt(ms + NORM_EPS) * (gain[...] * scale)
            blk = blk + aug[:, h * HEAD_PAD:(h + 1) * HEAD_PAD] + ones[...]
            out[0, h] = blk.astype(out.dtype)
    vt_out[0, :, 0] = _vt_tile(u[:, 2 * FOX_WIDTH:], FOX_HEADS)


N_SEG = 6
N_MLA = 12
N_FOX = 7


def _front_kernel(x_ref, g_ref, *refs):
    w_mla, w_fox, w_small, w_z, w_xbc, w_gate = refs[:N_SEG]
    mla_in = refs[N_SEG:N_SEG + N_MLA]
    fox_in = refs[N_SEG + N_MLA:N_SEG + N_MLA + N_FOX]
    outs = refs[N_SEG + N_MLA + N_FOX:]
    q_out, k_out, vt_out, qf_out, kf_out, vtf_out, dt_out, z_out, xbc_out, gate_out, carry = outs

    @pl.when(pl.program_id(1) == 0)
    def _():
        carry[...] = jnp.zeros_like(carry)

    x = x_ref[...]
    hn = (x * lax.rsqrt(jnp.mean(x * x, axis=-1, keepdims=True) + NORM_EPS) * g_ref[...]).astype(BF16)
    u_mla = _dot(hn, w_mla[...])
    u_fox = _dot(hn, w_fox[...])
    u_small = _dot(hn, w_small[...])
    dt_out[...] = u_small[:, LANES:]
    _mla_prep(u_mla, *mla_in, q_out, k_out, vt_out)
    _fox_prep(u_fox, u_small[:, :LANES], *fox_in, qf_out, kf_out, vtf_out, carry)
    for w_ref, o_ref in ((w_z, z_out), (w_xbc, xbc_out), (w_gate, gate_out)):
        n = w_ref.shape[1]
        for cc in range(0, n, IN_CH):
            w = min(IN_CH, n - cc)
            o_ref[:, cc:cc + w] = _dot(hn, w_ref[:, cc:cc + w]).astype(o_ref.dtype)


def _front(x2, g, weights, mla_in, fox_in, B, S, ts, tk):
    nb = S // ts
    T = B * S
    H = MLA_HEADS
    per_tile = tk // ts
    assert len(weights) == N_SEG and len(mla_in) == N_MLA and len(fox_in) == N_FOX and tk % ts == 0
    tok = lambda b, i: (b * nb + i, 0)
    const = lambda b, i: (0, 0)
    row = lambda w: pl.BlockSpec((1, w), const)
    tab = pl.BlockSpec((ts, LANES), tok)
    head_major = pl.BlockSpec((1, H, ts, HEAD_PAD), lambda b, i: (b, 0, i, 0))
    vt_spec = pl.BlockSpec((1, H, 1, V_ROWS, ts), lambda b, i: (b, 0, i // per_tile, 0, i % per_tile))
    qk_shape = jax.ShapeDtypeStruct((B, H, S, HEAD_PAD), BF16)
    vt_shape = jax.ShapeDtypeStruct((B, H, S // tk, V_ROWS, tk), BF16)
    wide = lambda n: (pl.BlockSpec((ts, n), tok), jax.ShapeDtypeStruct((T, n), BF16))
    z_spec, z_shape = wide(SSM_INNER)
    xbc_spec, xbc_shape = wide(SSM_XBC)
    gate_spec, gate_shape = wide(3 * D_MODEL)
    return pl.pallas_call(
        _front_kernel,
        grid=(B, nb),
        in_specs=[pl.BlockSpec((ts, D_MODEL), tok), row(D_MODEL)] + [_vmem_full()] * N_SEG
                 + [tab, tab, tab, row(MLA_Q_RANK), _vmem_full(), row(MLA_KV_RANK), _vmem_full(), _vmem_full(),
                    row(LANES), row(LANES), row(LANES), row(LANES)]
                 + [row(LANES), row(LANES), row(LANES), _vmem_full(), _vmem_full(), row(LANES), row(LANES)],
        out_specs=[head_major, head_major, vt_spec, head_major, head_major, vt_spec,
                   tab, z_spec, xbc_spec, gate_spec],
        out_shape=[qk_shape, qk_shape, vt_shape, qk_shape, qk_shape, vt_shape,
                   jax.ShapeDtypeStruct((T, LANES), F32), z_shape, xbc_shape, gate_shape],
        scratch_shapes=[pltpu.VMEM((1, LANES), F32)],
        compiler_params=_cparams(2),
        name="front",
    )(x2, g, *weights, *mla_in, *fox_in)


ATTN_TILE = 512
V_ROWS = 64


def _vt_tile(v_all, heads):
    ts = v_all.shape[0]
    return v_all.T.reshape(heads, V_ROWS, ts).astype(BF16)


HEADS_PER_STEP = 4
BOUND_SLACK = 1.02
BOUND_STEP = 0.25
MAX_SAFE_BOUND = 60.0


def _score_bound(q_gain, k_gain, dim):
    bound = dim * jnp.max(jnp.abs(q_gain)) * jnp.max(jnp.abs(k_gain)) * (dim ** -0.5) * LOG2E * BOUND_SLACK
    bound = jnp.ceil(bound.astype(F32) / BOUND_STEP) * BOUND_STEP
    safe = bound < MAX_SAFE_BOUND
    return jnp.where(safe, bound, 0.0), safe.astype(jnp.int32).reshape(1)


def _attn_kernel(flag_ref, q_ref, k_ref, vt_ref, o_ref, *, tq, chunk_causal):
    i = pl.program_id(2)
    hs = q_ref.shape[1]
    dv = vt_ref.shape[3]

    half = tq // 2
    q_t = [q_ref[0, hh].astype(F32).T.astype(BF16) for hh in range(hs)]

    def causal_mask(shape):
        r = lax.broadcasted_iota(jnp.int32, shape, 0)
        c = lax.broadcasted_iota(jnp.int32, shape, 1)
        shift = CHUNK.bit_length() - 1
        return ((r >> shift) <= (c >> shift)) if chunk_causal else (r <= c)

    def column_sums(p):
        return p.reshape(p.shape[0] // SUBLANES, SUBLANES, p.shape[1]).sum(axis=0)

    def pipeline(items, state):
        state = list(state)
        s_next = items[0][1]()
        for n, (hh, _, consume) in enumerate(items):
            s = s_next
            if n + 1 < len(items):
                s_next = items[n + 1][1]()
            state[hh] = consume(state[hh], s)
        return tuple(state)

    def full_items(js, update):
        items = []
        for j in js:
            for hh in range(hs):
                def scores(j=j, hh=hh):
                    kj = k_ref[0, hh, pl.ds(pl.multiple_of(j * tq, tq), tq), :]
                    return _dot(kj, q_t[hh])
                items.append((hh, scores, lambda st, s, j=j, hh=hh: update(st, s, vt_ref[0, hh, j])))
        return items

    def sweep(update, init, diagonal, finish):
        body = lambda n: (lambda jj, st: pipeline(full_items([n * jj + t for t in range(n)], update), st))
        quads = lax.shift_right_logical(i, 2)
        state = lax.fori_loop(0, quads, body(4), init)
        done = quads * 4
        state = lax.fori_loop(0, lax.shift_right_logical(i - done, 1),
                              lambda _, st: pipeline(full_items([done, done + 1], update), st), state)
        state = lax.fori_loop(0, i & 1, lambda _, st: pipeline(full_items([i - 1], update), st), state)
        state = diagonal(state)
        o = jnp.concatenate([finish(st) for st in state], axis=0)
        o_ref[0] = o.T.astype(o_ref.dtype)

    @pl.when(flag_ref[0] == 1)
    def _():
        def update(st, s, vt):
            l8, acc = st
            p = jnp.exp2(s)
            return l8 + column_sums(p), acc + _dot(vt, p.astype(BF16))

        def diagonal(state):
            ok_a = causal_mask((half, tq))
            ok_b = causal_mask((half, half))
            items = []
            for hh in range(hs):
                def scores_a(hh=hh):
                    ka = k_ref[0, hh, pl.ds(pl.multiple_of(i * tq, tq), half), :]
                    return _dot(ka, q_t[hh])
                def consume_a(st, s, hh=hh):
                    return update(st, jnp.where(ok_a, s, NEG_BIG), vt_ref[0, hh, i, :, :half])
                def scores_b(hh=hh):
                    kb = k_ref[0, hh, pl.ds(pl.multiple_of(i * tq + half, half), half), :]
                    return _dot(kb, q_t[hh][:, half:])
                def consume_b(st, s, hh=hh):
                    l8, acc = st
                    p = jnp.exp2(jnp.where(ok_b, s, NEG_BIG))
                    l8 = jnp.concatenate([l8[:, :half], l8[:, half:] + column_sums(p)], axis=1)
                    pv = _dot(vt_ref[0, hh, i, :, half:], p.astype(BF16))
                    return l8, jnp.concatenate([acc[:, :half], acc[:, half:] + pv], axis=1)
                items += [(hh, scores_a, consume_a), (hh, scores_b, consume_b)]
            return pipeline(items, state)

        init = tuple((jnp.zeros((SUBLANES, tq), F32), jnp.zeros((dv, tq), F32)) for _ in range(hs))
        sweep(update, init, diagonal, lambda st: st[1] * (1.0 / jnp.sum(st[0], axis=0, keepdims=True)))

    @pl.when(flag_ref[0] != 1)
    def _():
        def update(st, s, vt):
            m, l8, acc = st
            m_new = jnp.maximum(m, jnp.max(s, axis=0, keepdims=True))
            alpha = jnp.exp2(m - m_new)
            p = jnp.exp2(s - m_new)
            return m_new, alpha * l8 + column_sums(p), alpha * acc + _dot(vt, p.astype(BF16))

        def diagonal(state):
            ok = causal_mask((tq, tq))
            masked = lambda st, s, vt: update(st, jnp.where(ok, s, NEG_BIG), vt)
            return pipeline(full_items([i], masked), state)

        init = tuple((jnp.full((1, tq), NEG_BIG, F32), jnp.zeros((SUBLANES, tq), F32),
                      jnp.zeros((dv, tq), F32)) for _ in range(hs))
        sweep(update, init, diagonal, lambda st: st[2] * (1.0 / jnp.sum(st[1], axis=0, keepdims=True)))


def _attention(flag, q, k, vt, tq, chunk_causal):
    B, H, S, _ = q.shape
    dv = vt.shape[3]
    hs = HEADS_PER_STEP
    assert vt.shape[4] == tq and (hs * dv) % LANES == 0 and H % hs == 0
    nq = S // tq
    return pl.pallas_call(
        functools.partial(_attn_kernel, tq=tq, chunk_causal=chunk_causal),
        grid=(B, H // hs, nq),
        in_specs=[pl.BlockSpec(memory_space=pltpu.SMEM),
                  pl.BlockSpec((1, hs, tq, HEAD_PAD), lambda b, h, i: (b, h, i, 0)),
                  pl.BlockSpec((1, hs, S, HEAD_PAD), lambda b, h, i: (b, h, 0, 0)),
                  pl.BlockSpec((1, hs, nq, dv, tq), lambda b, h, i: (b, h, 0, 0, 0))],
        out_specs=pl.BlockSpec((1, tq, hs * dv), lambda b, h, i: (b, i, h)),
        out_shape=jax.ShapeDtypeStruct((B, S, H * dv), BF16),
        compiler_params=_cparams(3),
        name="attn_mla" if chunk_causal else "attn_fox",
    )(flag, q, k, vt)


def _ssm_kernel(z_ref, xbc_ref, dt_ref, cw_ref, cb_ref, dtb_ref, alog_ref, dx_ref, ng_ref, e_ref,
                o_ref, xpad, hst, *, lc):
    pad = SUBLANES

    @pl.when(pl.program_id(1) == 0)
    def _():
        xpad[0:pad, :] = jnp.zeros((pad, SSM_XBC), F32)
        hst[...] = jnp.zeros_like(hst)

    xpad[pad:pad + lc, :] = xbc_ref[...].astype(F32)
    xp = xpad[...]
    conv = cb_ref[...] + cw_ref[SSM_CONV - 1:SSM_CONV, :] * xp[pad:, :]
    for back in range(1, SSM_CONV):
        kk = SSM_CONV - 1 - back
        conv = conv + cw_ref[kk:kk + 1, :] * pltpu.roll(xp, back, axis=0)[pad:, :]
    xpad[0:pad, :] = xpad[lc:lc + pad, :]
    xc = _silu(conv)
    xs = xc[:, :SSM_INNER]
    bm = xc[:, SSM_INNER:SSM_INNER + SSM_GROUPS * SSM_STATE]
    cm = xc[:, SSM_INNER + SSM_GROUPS * SSM_STATE:]

    lane = lax.broadcasted_iota(jnp.int32, (lc, LANES), 1)
    dt = jnp.where(lane < SSM_HEADS, _softplus(dt_ref[...] + dtb_ref[...]), 0.0)
    a_neg = -jnp.exp(alog_ref[...]) * LOG2E
    a_cum = _dot3_left(_tril_bf16(lc), dt * a_neg)
    head_lanes = lane < SSM_HEADS
    ea = jnp.where(head_lanes, jnp.exp2(a_cum), 0.0)
    dte = jnp.where(head_lanes, jnp.exp2(a_cum[lc - 1:lc, :] - a_cum), 0.0)
    e01 = e_ref[...]

    def expand(v):
        hi, mid, lo = _split3(v)
        packed = (hi.astype(F32) + pltpu.roll(mid.astype(F32), SSM_HEADS, axis=1)
                  + pltpu.roll(lo.astype(F32), 2 * SSM_HEADS, axis=1)).astype(BF16)
        return _dot(packed, e01)

    dt_x = expand(dt)
    ea_x = expand(ea)
    dte_x = expand(dte)
    xdt = xs * dt_x
    a_t = a_cum.T

    row = lax.broadcasted_iota(jnp.int32, (lc, lc), 0)
    col = lax.broadcasted_iota(jnp.int32, (lc, lc), 1)
    causal = row >= col
    lane_first = lane < SSM_HEAD_DIM
    heads_per_group = SSM_HEADS // SSM_GROUPS
    y_blocks = []
    for g in range(SSM_GROUPS):
        bg = bm[:, g * SSM_STATE:(g + 1) * SSM_STATE]
        cg = cm[:, g * SSM_STATE:(g + 1) * SSM_STATE].astype(BF16)
        cb = _dot_nt(cg, bg.astype(BF16))
        gs = slice(g * SSM_GROUP_WIDTH, (g + 1) * SSM_GROUP_WIDTH)
        h_prev = hst[g]
        y_off = _dot(cg, h_prev.astype(BF16)) * ea_x[:, gs]
        xw = (xdt[:, gs] * dte_x[:, gs]).astype(BF16)
        s_new = _dot(bg.T.astype(BF16), xw)
        hst[g] = h_prev * ea_x[lc - 1:lc, gs] + s_new
        for hp in range(heads_per_group // 2):
            t0 = g * SSM_GROUP_WIDTH + hp * LANES
            x_pair = xdt[:, t0:t0 + LANES]
            acc = None
            for e in range(2):
                h = g * heads_per_group + hp * 2 + e
                seg = a_cum[:, h:h + 1] - a_t[h:h + 1, :]
                decay = jnp.exp2(jnp.where(causal, seg, NEG_BIG))
                gmat = (cb * decay).astype(BF16)
                xh = jnp.where(lane_first if e == 0 else ~lane_first, x_pair, 0.0).astype(BF16)
                part = _dot(gmat, xh)
                acc = part if acc is None else acc + part
            y_blocks.append(acc + y_off[:, hp * LANES:(hp + 1) * LANES])
    y = jnp.concatenate(y_blocks, axis=1) + xs * dx_ref[...]
    yg = y * _silu(z_ref[...].astype(F32))
    o_ref[...] = (yg * lax.rsqrt(jnp.mean(yg * yg, axis=-1, keepdims=True) + NORM_EPS)
                  * ng_ref[...]).astype(o_ref.dtype)


def _ssm(u_z, u_xbc, u_dt, cw, cb, dtb, alog, dx, ng, e01, B, S, lc):
    nb = S // lc
    tok = lambda b, i: (b * nb + i, 0)
    const = lambda b, i: (0, 0)
    return pl.pallas_call(
        functools.partial(_ssm_kernel, lc=lc),
        grid=(B, nb),
        in_specs=[pl.BlockSpec((lc, SSM_INNER), tok),
                  pl.BlockSpec((lc, SSM_XBC), tok),
                  pl.BlockSpec((lc, LANES), tok),
                  pl.BlockSpec((SSM_CONV, SSM_XBC), const), pl.BlockSpec((1, SSM_XBC), const),
                  pl.BlockSpec((1, LANES), const), pl.BlockSpec((1, LANES), const),
                  pl.BlockSpec((1, SSM_INNER), const), pl.BlockSpec((1, SSM_INNER), const),
                  _vmem_full()],
        out_specs=pl.BlockSpec((lc, SSM_INNER), tok),
        out_shape=jax.ShapeDtypeStruct((B * S, SSM_INNER), BF16),
        scratch_shapes=[pltpu.VMEM((lc + SUBLANES, SSM_XBC), F32),
                        pltpu.VMEM((SSM_GROUPS, SSM_STATE, SSM_GROUP_WIDTH), F32)],
        compiler_params=_cparams(2),
        name="ssm",
    )(u_z, u_xbc, u_dt, cw, cb, dtb, alog, dx, ng, e01)


def _merge_kernel(x_ref, ya_ref, yb_ref, yc_ref, ug_ref, bg_ref, wa_ref, wb_ref, wc_ref, wo_ref, o_ref):
    ys = (_dot(ya_ref[...], wa_ref[...]), _dot(yb_ref[...], wb_ref[...]), _dot(yc_ref[...], wc_ref[...]))
    merged = None
    for n, y in enumerate(ys):
        cs = slice(n * D_MODEL, (n + 1) * D_MODEL)
        gate = _sigmoid(ug_ref[:, cs].astype(F32) + bg_ref[:, cs])
        merged = gate * y if merged is None else merged + gate * y
    o_ref[...] = x_ref[...] + _dot(merged.astype(BF16), wo_ref[...])


def _merge(x2, ya, yb, yc, ug, bg, wa, wb, wc, wo, tm):
    T = x2.shape[0]
    row = lambda w: pl.BlockSpec((tm, w), lambda i: (i, 0))
    return pl.pallas_call(
        _merge_kernel,
        grid=(T // tm,),
        in_specs=[row(D_MODEL), row(MLA_HEADS * MLA_V), row(FOX_WIDTH), row(SSM_INNER), row(3 * D_MODEL),
                  pl.BlockSpec((1, 3 * D_MODEL), lambda i: (0, 0)),
                  _vmem_full(), _vmem_full(), _vmem_full(), _vmem_full()],
        out_specs=row(D_MODEL),
        out_shape=jax.ShapeDtypeStruct((T, D_MODEL), F32),
        compiler_params=_cparams(1),
        name="merge",
    )(x2, ya, yb, yc, ug, bg, wa, wb, wc, wo)


FF_CH = 256


def _ffn_kernel(x_ref, g_ref, wup_ref, cw_ref, cb_ref, wdn_ref, o_ref, abuf, hbuf, *, tm, blocks_per_seq):
    pad = SUBLANES

    @pl.when(pl.program_id(0) % blocks_per_seq == 0)
    def _():
        abuf[0:pad, :] = jnp.zeros((pad, 2 * D_FF), F32)

    x = x_ref[...]
    hn = (x * lax.rsqrt(jnp.mean(x * x, axis=-1, keepdims=True) + NORM_EPS) * g_ref[...]).astype(BF16)
    for c0 in range(0, D_FF, FF_CH):
        halves = []
        for base in (c0, D_FF + c0):
            cs = slice(base, base + FF_CH)
            abuf[pad:pad + tm, cs] = _dot(hn, wup_ref[:, cs])
            ap = abuf[:, cs]
            conv = cb_ref[:, cs] + cw_ref[FFN_CONV - 1:FFN_CONV, cs] * ap[pad:, :]
            for back in range(1, FFN_CONV):
                kk = FFN_CONV - 1 - back
                conv = conv + cw_ref[kk:kk + 1, cs] * pltpu.roll(ap, back, axis=0)[pad:, :]
            abuf[0:pad, cs] = abuf[tm:tm + pad, cs]
            halves.append(conv)
        hbuf[:, c0:c0 + FF_CH] = (_silu(halves[0]) * halves[1]).astype(BF16)
    o_ref[...] = x + _dot(hbuf[...], wdn_ref[...])


def _ffn(x2, g, wup, cw, cb, wdn, S, tm):
    T = x2.shape[0]
    return pl.pallas_call(
        functools.partial(_ffn_kernel, tm=tm, blocks_per_seq=S // tm),
        grid=(T // tm,),
        in_specs=[pl.BlockSpec((tm, D_MODEL), lambda i: (i, 0)),
                  pl.BlockSpec((1, D_MODEL), lambda i: (0, 0)),
                  _vmem_full(),
                  pl.BlockSpec((FFN_CONV, 2 * D_FF), lambda i: (0, 0)),
                  pl.BlockSpec((1, 2 * D_FF), lambda i: (0, 0)),
                  _vmem_full()],
        out_specs=pl.BlockSpec((tm, D_MODEL), lambda i: (i, 0)),
        out_shape=jax.ShapeDtypeStruct((T, D_MODEL), F32),
        scratch_shapes=[pltpu.VMEM((tm + SUBLANES, 2 * D_FF), F32),
                        pltpu.VMEM((tm, D_FF), BF16)],
        compiler_params=_cparams(1),
        name="ffn",
    )(x2, g, wup, cw, cb, wdn)


def _pad_lanes(v, width=LANES):
    return jnp.pad(v.astype(F32), (0, width - v.shape[0])).reshape(1, width)


def _pack_w_in(w):
    zeros = lambda n: jnp.zeros((D_MODEL, n), BF16)
    seg = lambda a, n: w[:, a:a + n].astype(BF16)
    o_fox = MLA_IN
    o_ssm = MLA_IN + FOX_IN
    o_gate = o_ssm + SSM_IN
    kr0 = MLA_Q_RANK + MLA_KV_RANK
    w_mla = jnp.concatenate([seg(0, kr0), zeros(MLA_NOPE), seg(kr0, MLA_ROPE), zeros(LANES - MLA_QK_DIM)],
                            axis=1)
    w_small = jnp.concatenate([seg(o_fox + 3 * FOX_WIDTH, FOX_HEADS), zeros(LANES - FOX_HEADS),
                               seg(o_ssm + SSM_INNER + SSM_XBC, SSM_HEADS), zeros(LANES - SSM_HEADS)], axis=1)
    assert w_mla.shape[1] == MLA_SEG and w_small.shape[1] == SMALL_W
    return [w_mla, seg(o_fox, 3 * FOX_WIDTH), w_small, seg(o_ssm, SSM_INNER), seg(o_ssm + SSM_INNER, SSM_XBC),
            seg(o_gate, 3 * D_MODEL)]


def _one_hot_row(lanes, value=1.0):
    row = np.zeros((1, LANES), np.float32)
    row[0, lanes] = value
    return row


def _fox_placement():
    rows = np.arange(3 * FOX_HEADS)
    piece = rows // FOX_HEADS
    head = rows % FOX_HEADS
    pq = np.zeros((LANES, FOX_HEADS * HEAD_PAD), np.float32)
    pk = np.zeros((LANES, FOX_HEADS * HEAD_PAD), np.float32)
    pq[rows, head * HEAD_PAD + AUG0 + piece] = 1.0
    pk[rows, head * HEAD_PAD + AUG0 + 3 + piece] = -1.0
    oneq = _one_hot_row(slice(AUG0 + 3, AUG0 + 6))
    onek = _one_hot_row(slice(AUG0, AUG0 + 3))
    return jnp.asarray(pq, BF16), jnp.asarray(pk, BF16), jnp.asarray(oneq), jnp.asarray(onek)


def _shift_rows(bound, lane):
    return -bound * jnp.asarray(_one_hot_row(lane)), jnp.asarray(_one_hot_row(lane))


def _head_expand():
    h = np.arange(3 * SSM_HEADS)
    e = np.zeros((LANES, SSM_INNER), np.float32)
    cols = (h % SSM_HEADS)[:, None] * SSM_HEAD_DIM + np.arange(SSM_HEAD_DIM)[None, :]
    e[h[:, None], cols] = 1.0
    return jnp.asarray(e, BF16)


def kernel(x, positions, norm_mix_g, w_in, b_gate, mla_q_norm_g, mla_w_uq, mla_kv_norm_g, mla_w_ukv,
           mla_q_gain, mla_k_gain, fox_q_gain, fox_k_gain, fox_b_f, ssm_conv_w, ssm_conv_b, ssm_dt_bias,
           ssm_A_log, ssm_D, ssm_norm_g, w_br_mla, w_br_fox, w_br_ssm, w_out, norm_ffn_g, ffn_w_up,
           ffn_conv_w, ffn_conv_b, ffn_w_down):
    B, S, D = x.shape
    T = B * S
    depth = w_in.shape[0]
    tile = min(256, S)
    td = min(DENSE_TILE, S)
    ta = min(ATTN_TILE, S)

    pos = positions.reshape(T, 1).astype(F32)
    cos, sa, sb = _rope_tables(pos, min(ROPE_TILE, S))
    pq, pk, oneq, onek = _fox_placement()
    e01 = _head_expand()
    row = lambda v: v.reshape(1, -1).astype(F32)

    x2 = x.reshape(T, D).astype(F32)
    for l in range(depth):
        wq = jnp.pad(mla_w_uq[l].reshape(MLA_Q_RANK, MLA_HEADS, MLA_QK_DIM),
                     ((0, 0), (0, 0), (0, HEAD_PAD - MLA_QK_DIM))).reshape(MLA_Q_RANK, -1).astype(BF16)
        wkv = mla_w_ukv[l].reshape(MLA_KV_RANK, MLA_HEADS, MLA_NOPE + MLA_V)
        wk = jnp.pad(wkv[:, :, :MLA_NOPE], ((0, 0), (0, 0), (0, HEAD_PAD - MLA_NOPE))
                     ).reshape(MLA_KV_RANK, -1).astype(BF16)
        wv = wkv[:, :, MLA_NOPE:].reshape(MLA_KV_RANK, -1).astype(BF16)
        bound, flag_a = _score_bound(mla_q_gain[l], mla_k_gain[l], MLA_QK_DIM)
        qfix, kfix = _shift_rows(bound, MLA_QK_DIM)
        mla_in = [cos, sa, sb, row(mla_q_norm_g[l]), wq, row(mla_kv_norm_g[l]), wk, wv,
                  _pad_lanes(mla_q_gain[l]), _pad_lanes(mla_k_gain[l]), qfix, kfix]
        bound, flag_b = _score_bound(fox_q_gain[l], fox_k_gain[l], FOX_HEAD_DIM)
        qfix, kfix = _shift_rows(bound, AUG0 + 6)
        fox_in = [_pad_lanes(fox_b_f[l]), _pad_lanes(fox_q_gain[l]), _pad_lanes(fox_k_gain[l]), pq, pk,
                  oneq + qfix, onek + kfix]
        q, k, vt, qf, kf, vtf, u_dt, u_z, u_xbc, u_gate = _front(
            x2, row(norm_mix_g[l]), _pack_w_in(w_in[l]), mla_in, fox_in, B, S, tile, ta)
        y_a = _attention(flag_a, q, k, vt, ta, True).reshape(T, -1)
        y_b = _attention(flag_b, qf, kf, vtf, ta, False).reshape(T, -1)

        dx = jnp.repeat(ssm_D[l].astype(F32), SSM_HEAD_DIM).reshape(1, -1)
        y_c = _ssm(u_z, u_xbc, u_dt, ssm_conv_w[l].astype(F32), row(ssm_conv_b[l]),
                   _pad_lanes(ssm_dt_bias[l]), _pad_lanes(ssm_A_log[l]), dx, row(ssm_norm_g[l]), e01,
                   B, S, tile)

        x2 = _merge(x2, y_a, y_b, y_c, u_gate, row(b_gate[l]), w_br_mla[l].astype(BF16),
                    w_br_fox[l].astype(BF16), w_br_ssm[l].astype(BF16), w_out[l].astype(BF16), td)
        x2 = _ffn(x2, row(norm_ffn_g[l]), ffn_w_up[l].astype(BF16), ffn_conv_w[l].astype(F32),
                  row(ffn_conv_b[l]), ffn_w_down[l].astype(BF16), S, td)
    return x2.reshape(B, S, D).astype(x.dtype)
```

```python
import functools

import numpy as np
import jax
import jax.numpy as jnp
from jax import lax
from jax.experimental import pallas as pl
from jax.experimental.pallas import tpu as pltpu

F32 = jnp.float32
BF16 = jnp.bfloat16

D_MODEL = 1024
NORM_EPS = 1e-6
CHUNK = 64

MLA_HEADS = 8
MLA_Q_RANK = 384
MLA_KV_RANK = 256
MLA_NOPE = 64
MLA_ROPE = 32
MLA_QK_DIM = MLA_NOPE + MLA_ROPE
MLA_V = 64
ROPE_THETA = 10000.0

FOX_HEADS = 8
FOX_HEAD_DIM = 64
FOX_WIDTH = FOX_HEADS * FOX_HEAD_DIM

SSM_HEADS = 16
SSM_HEAD_DIM = 64
SSM_INNER = SSM_HEADS * SSM_HEAD_DIM
SSM_GROUPS = 2
SSM_STATE = 128
SSM_CONV = 4
SSM_XBC = SSM_INNER + 2 * SSM_GROUPS * SSM_STATE
SSM_GROUP_WIDTH = SSM_INNER // SSM_GROUPS

D_FF = 2816
FFN_CONV = 3

MLA_IN = MLA_Q_RANK + MLA_KV_RANK + MLA_ROPE
FOX_IN = 3 * FOX_WIDTH + FOX_HEADS
SSM_IN = SSM_INNER + SSM_XBC + SSM_HEADS

LANES = 128
SUBLANES = 8
HEAD_PAD = LANES
LOG2E = 1.4426950408889634
NEG_BIG = -1e30
VMEM_LIMIT = 52 * 1024 * 1024

MLA_SEG = MLA_Q_RANK + MLA_KV_RANK + LANES
SMALL_W = 2 * LANES

AUG0 = FOX_HEAD_DIM


def _cparams(n_grid):
    return pltpu.CompilerParams(dimension_semantics=("arbitrary",) * n_grid,
                                vmem_limit_bytes=VMEM_LIMIT)


def _vmem_full():
    return pl.BlockSpec(memory_space=pltpu.VMEM)


def _dot(a, b):
    return jnp.dot(a, b, preferred_element_type=F32)


def _dot_nt(a, b):
    return lax.dot_general(a, b, (((1,), (1,)), ((), ())), preferred_element_type=F32)


def _split3(x):
    hi = x.astype(BF16)
    r = x - hi.astype(F32)
    mid = r.astype(BF16)
    lo = (r - mid.astype(F32)).astype(BF16)
    return hi, mid, lo


def _dot3_left(m01, x):
    hi, mid, lo = _split3(x)
    return _dot(m01, hi) + _dot(m01, mid) + _dot(m01, lo)


def _dot3_right(x, m01):
    hi, mid, lo = _split3(x)
    return _dot(hi, m01) + _dot(mid, m01) + _dot(lo, m01)


def _sigmoid(x):
    return 1.0 / (1.0 + jnp.exp(-x))


def _silu(x):
    return x * _sigmoid(x)


def _softplus(x):
    return jnp.maximum(x, 0.0) + jnp.log(1.0 + jnp.exp(-jnp.abs(x)))


def _tril_bf16(n):
    r = lax.broadcasted_iota(jnp.int32, (n, n), 0)
    c = lax.broadcasted_iota(jnp.int32, (n, n), 1)
    return jnp.where(r >= c, 1.0, 0.0).astype(BF16)


ROPE_TILE = 1024

def _rope_kernel(pos_ref, inv_ref, cos_ref, sa_ref, sb_ref):
    ang = pos_ref[...] * inv_ref[...]
    lane = lax.broadcasted_iota(jnp.int32, ang.shape, 1)
    c = jnp.cos(ang)
    s = jnp.sin(ang)
    half = MLA_ROPE // 2
    first = (lane >= MLA_NOPE) & (lane < MLA_NOPE + half)
    second = (lane >= MLA_NOPE + half) & (lane < MLA_QK_DIM)
    cos_ref[...] = jnp.where(lane < MLA_NOPE, 1.0, jnp.where(first | second, c, 0.0))
    sa_ref[...] = jnp.where(first, -s, 0.0)
    sb_ref[...] = jnp.where(second, s, 0.0)


def _rope_tables(pos, ts):
    T = pos.shape[0]
    half = MLA_ROPE // 2
    inv = 1.0 / (ROPE_THETA ** (jnp.arange(0, MLA_ROPE, 2, dtype=F32) / MLA_ROPE))
    inv_lane = jnp.concatenate([jnp.zeros((MLA_NOPE,), F32), inv, inv,
                                jnp.zeros((LANES - MLA_QK_DIM,), F32)]).reshape(1, LANES)
    tab = jax.ShapeDtypeStruct((T, LANES), F32)
    spec = pl.BlockSpec((ts, LANES), lambda i: (i, 0))
    return pl.pallas_call(
        _rope_kernel,
        grid=(T // ts,),
        in_specs=[pl.BlockSpec((ts, 1), lambda i: (i, 0)), pl.BlockSpec((1, LANES), lambda i: (0, 0))],
        out_specs=[spec, spec, spec],
        out_shape=[tab, tab, tab],
        compiler_params=_cparams(1),
        name="rope_tables",
    )(pos, inv_lane)


def _rope(x, cos, sa, sb):
    half = MLA_ROPE // 2
    return x * cos + pltpu.roll(x, LANES - half, axis=1) * sa + pltpu.roll(x, half, axis=1) * sb


IN_CH = 512
DENSE_TILE = 512


def _mla_prep(u, cos_ref, sa_ref, sb_ref, gq_ref, wq_ref, gkv_ref, wk_ref, wv_ref,
              qg_ref, kg_ref, qfix_ref, kfix_ref, q_out, k_out, vt_out):
    cos = cos_ref[...]
    sa = sa_ref[...]
    sb = sb_ref[...]
    cq = u[:, :MLA_Q_RANK]
    ckv = u[:, MLA_Q_RANK:MLA_Q_RANK + MLA_KV_RANK]
    kr = u[:, MLA_Q_RANK + MLA_KV_RANK:]
    cqn = (cq * lax.rsqrt(jnp.mean(cq * cq, axis=-1, keepdims=True) + NORM_EPS) * gq_ref[...]).astype(BF16)
    ckvn = (ckv * lax.rsqrt(jnp.mean(ckv * ckv, axis=-1, keepdims=True) + NORM_EPS) * gkv_ref[...]).astype(BF16)
    q_all = _dot(cqn, wq_ref[...])
    k_all = _dot(ckvn, wk_ref[...])
    v_all = _dot(ckvn, wv_ref[...])
    kr_ss = jnp.sum(kr * kr, axis=-1, keepdims=True)
    kr_rot = _rope(kr * kg_ref[...], cos, sa, sb)
    q_scale = (MLA_QK_DIM ** -0.5) * LOG2E
    for h in range(MLA_HEADS):
        qh = q_all[:, h * HEAD_PAD:(h + 1) * HEAD_PAD]
        ms = jnp.sum(qh * qh, axis=-1, keepdims=True) * (1.0 / MLA_QK_DIM)
        qh = qh * lax.rsqrt(ms + NORM_EPS) * qg_ref[...]
        q_out[0, h] = (_rope(qh, cos, sa, sb) * q_scale + qfix_ref[...]).astype(q_out.dtype)
        kn = k_all[:, h * HEAD_PAD:(h + 1) * HEAD_PAD]
        ms = (jnp.sum(kn * kn, axis=-1, keepdims=True) + kr_ss) * (1.0 / MLA_QK_DIM)
        kh = (kn * kg_ref[...] + kr_rot) * lax.rsqrt(ms + NORM_EPS)
        k_out[0, h] = (kh + kfix_ref[...]).astype(k_out.dtype)
    vt_out[0, :, 0] = _vt_tile(v_all, MLA_HEADS)


def _fox_prep(u, f_raw, bf_ref, qg_ref, kg_ref, pq_ref, pk_ref, oneq_ref, onek_ref,
              q_out, k_out, vt_out, carry):
    ts = u.shape[0]
    lane = lax.broadcasted_iota(jnp.int32, (ts, LANES), 1)
    f = f_raw + bf_ref[...]
    log_f = jnp.minimum(f, 0.0) - jnp.log(1.0 + jnp.exp(-jnp.abs(f)))
    log_f = jnp.where(lane < FOX_HEADS, log_f * LOG2E, 0.0)
    cum = carry[...] + _dot3_left(_tril_bf16(ts), log_f)
    carry[...] = cum[ts - 1:ts, :]
    hi, mid, lo = _split3(cum)
    packed = (hi.astype(F32) + pltpu.roll(mid.astype(F32), FOX_HEADS, axis=1)
              + pltpu.roll(lo.astype(F32), 2 * FOX_HEADS, axis=1)).astype(BF16)
    aug_q = _dot(packed, pq_ref[...])
    aug_k = _dot(packed, pk_ref[...])

    q_scale = (FOX_HEAD_DIM ** -0.5) * LOG2E
    head_lanes = lane < FOX_HEAD_DIM
    for h in range(FOX_HEADS):
        t0 = (h // 2) * LANES
        for src, gain, aug, ones, scale, out in (
                (0, qg_ref, aug_q, oneq_ref, q_scale, q_out),
                (FOX_WIDTH, kg_ref, aug_k, onek_ref, 1.0, k_out)):
            blk = u[:, src + t0:src + t0 + LANES]
            if h % 2 == 1:
                blk = pltpu.roll(blk, FOX_HEAD_DIM, axis=1)
            blk = jnp.where(head_lanes, blk, 0.0)
            ms = jnp.sum(blk * blk, axis=-1, keepdims=True) * (1.0 / FOX_HEAD_DIM)
            blk = blk * lax.rsqrt(ms + NORM_EPS) * (gain[...] * scale)
            blk = blk + aug[:, h * HEAD_PAD:(h + 1) * HEAD_PAD] + ones[...]
            out[0, h] = blk.astype(out.dtype)
    vt_out[0, :, 0] = _vt_tile(u[:, 2 * FOX_WIDTH:], FOX_HEADS)


N_SEG = 6
N_MLA = 12
N_FOX = 7


def _front_kernel(x_ref, g_ref, *refs):
    w_mla, w_fox, w_small, w_z, w_xbc, w_gate = refs[:N_SEG]
    mla_in = refs[N_SEG:N_SEG + N_MLA]
    fox_in = refs[N_SEG + N_MLA:N_SEG + N_MLA + N_FOX]
    outs = refs[N_SEG + N_MLA + N_FOX:]
    q_out, k_out, vt_out, qf_out, kf_out, vtf_out, dt_out, z_out, xbc_out, gate_out, carry = outs

    @pl.when(pl.program_id(1) == 0)
    def _():
        carry[...] = jnp.zeros_like(carry)

    x = x_ref[...]
    hn = (x * lax.rsqrt(jnp.mean(x * x, axis=-1, keepdims=True) + NORM_EPS) * g_ref[...]).astype(BF16)
    u_mla = _dot(hn, w_mla[...])
    u_fox = _dot(hn, w_fox[...])
    u_small = _dot(hn, w_small[...])
    dt_out[...] = u_small[:, LANES:]
    _mla_prep(u_mla, *mla_in, q_out, k_out, vt_out)
    _fox_prep(u_fox, u_small[:, :LANES], *fox_in, qf_out, kf_out, vtf_out, carry)
    for w_ref, o_ref in ((w_z, z_out), (w_xbc, xbc_out), (w_gate, gate_out)):
        n = w_ref.shape[1]
        for cc in range(0, n, IN_CH):
            w = min(IN_CH, n - cc)
            o_ref[:, cc:cc + w] = _dot(hn, w_ref[:, cc:cc + w]).astype(o_ref.dtype)


def _front(x2, g, weights, mla_in, fox_in, B, S, ts, tk):
    nb = S // ts
    T = B * S
    H = MLA_HEADS
    per_tile = tk // ts
    assert len(weights) == N_SEG and len(mla_in) == N_MLA and len(fox_in) == N_FOX and tk % ts == 0
    tok = lambda b, i: (b * nb + i, 0)
    const = lambda b, i: (0, 0)
    row = lambda w: pl.BlockSpec((1, w), const)
    tab = pl.BlockSpec((ts, LANES), tok)
    head_major = pl.BlockSpec((1, H, ts, HEAD_PAD), lambda b, i: (b, 0, i, 0))
    vt_spec = pl.BlockSpec((1, H, 1, V_ROWS, ts), lambda b, i: (b, 0, i // per_tile, 0, i % per_tile))
    qk_shape = jax.ShapeDtypeStruct((B, H, S, HEAD_PAD), BF16)
    vt_shape = jax.ShapeDtypeStruct((B, H, S // tk, V_ROWS, tk), BF16)
    wide = lambda n: (pl.BlockSpec((ts, n), tok), jax.ShapeDtypeStruct((T, n), BF16))
    z_spec, z_shape = wide(SSM_INNER)
    xbc_spec, xbc_shape = wide(SSM_XBC)
    gate_spec, gate_shape = wide(3 * D_MODEL)
    return pl.pallas_call(
        _front_kernel,
        grid=(B, nb),
        in_specs=[pl.BlockSpec((ts, D_MODEL), tok), row(D_MODEL)] + [_vmem_full()] * N_SEG
                 + [tab, tab, tab, row(MLA_Q_RANK), _vmem_full(), row(MLA_KV_RANK), _vmem_full(), _vmem_full(),
                    row(LANES), row(LANES), row(LANES), row(LANES)]
                 + [row(LANES), row(LANES), row(LANES), _vmem_full(), _vmem_full(), row(LANES), row(LANES)],
        out_specs=[head_major, head_major, vt_spec, head_major, head_major, vt_spec,
                   tab, z_spec, xbc_spec, gate_spec],
        out_shape=[qk_shape, qk_shape, vt_shape, qk_shape, qk_shape, vt_shape,
                   jax.ShapeDtypeStruct((T, LANES), F32), z_shape, xbc_shape, gate_shape],
        scratch_shapes=[pltpu.VMEM((1, LANES), F32)],
        compiler_params=_cparams(2),
        name="front",
    )(x2, g, *weights, *mla_in, *fox_in)


ATTN_TILE = 512
V_ROWS = 64


def _vt_tile(v_all, heads):
    ts = v_all.shape[0]
    return v_all.T.reshape(heads, V_ROWS, ts).astype(BF16)


HEADS_PER_STEP = 4
KEY_SPLIT = 1
BOUND_SLACK = 1.02
BOUND_STEP = 0.25
MAX_SAFE_BOUND = 60.0


def _score_bound(q_gain, k_gain, dim):
    bound = dim * jnp.max(jnp.abs(q_gain)) * jnp.max(jnp.abs(k_gain)) * (dim ** -0.5) * LOG2E * BOUND_SLACK
    bound = jnp.ceil(bound.astype(F32) / BOUND_STEP) * BOUND_STEP
    safe = bound < MAX_SAFE_BOUND
    return jnp.where(safe, bound, 0.0), safe.astype(jnp.int32).reshape(1)


def _attn_kernel(flag_ref, q_ref, k_ref, vt_ref, o_ref, *, tq, chunk_causal):
    i = pl.program_id(2)
    hs = q_ref.shape[1]
    dv = vt_ref.shape[3]

    half = tq // 2
    q_t = [q_ref[0, hh].astype(F32).T.astype(BF16) for hh in range(hs)]

    def causal_mask(shape, first_key=0):
        r = lax.broadcasted_iota(jnp.int32, shape, 0) + first_key
        c = lax.broadcasted_iota(jnp.int32, shape, 1)
        shift = CHUNK.bit_length() - 1
        return ((r >> shift) <= (c >> shift)) if chunk_causal else (r <= c)

    def column_sums(p):
        return p.reshape(p.shape[0] // SUBLANES, SUBLANES, p.shape[1]).sum(axis=0)

    def pipeline(items, state):
        state = list(state)
        s_next = items[0][1]()
        for n, (hh, _, consume) in enumerate(items):
            s = s_next
            if n + 1 < len(items):
                s_next = items[n + 1][1]()
            state[hh] = consume(state[hh], s)
        return tuple(state)

    def full_items(js, update):
        items = []
        span = 2 if len(js) % 2 == 0 else 1
        for j in js[::span]:
            for hh in range(hs):
                def scores(j=j, hh=hh):
                    kj = k_ref[0, hh, pl.ds(pl.multiple_of(j * tq, tq), span * tq), :]
                    return _dot(kj, q_t[hh])
                def consume(st, s, j=j, hh=hh):
                    for t in range(span):
                        st = update(st, s[t * tq:(t + 1) * tq], vt_ref[0, hh, j + t])
                    return st
                items.append((hh, scores, consume))
        return items

    def sweep(update, init, diagonal, finish):
        body = lambda n: (lambda jj, st: pipeline(full_items([n * jj + t for t in range(n)], update), st))
        quads = lax.shift_right_logical(i, 2)
        state = lax.fori_loop(0, quads, body(4), init)
        done = quads * 4
        state = lax.fori_loop(0, lax.shift_right_logical(i - done, 1),
                              lambda _, st: pipeline(full_items([done, done + 1], update), st), state)
        state = lax.fori_loop(0, i & 1, lambda _, st: pipeline(full_items([i - 1], update), st), state)
        state = diagonal(state)
        o = jnp.concatenate([finish(st) for st in state], axis=0)
        o_ref[0] = o.T.astype(o_ref.dtype)

    @pl.when(flag_ref[0] == 1)
    def _():
        def update(st, s, vt):
            l8, acc = st
            p = jnp.exp2(s)
            return l8 + column_sums(p), acc + _dot(vt, p.astype(BF16))

        def diagonal(state):
            ok_a = causal_mask((half, tq))
            ok_b = causal_mask((half, half))
            items = []
            for hh in range(hs):
                def scores_a(hh=hh):
                    ka = k_ref[0, hh, pl.ds(pl.multiple_of(i * tq, tq), half), :]
                    return _dot(ka, q_t[hh])
                def consume_a(st, s, hh=hh):
                    return update(st, jnp.where(ok_a, s, NEG_BIG), vt_ref[0, hh, i, :, :half])
                def scores_b(hh=hh):
                    kb = k_ref[0, hh, pl.ds(pl.multiple_of(i * tq + half, half), half), :]
                    return _dot(kb, q_t[hh][:, half:])
                def consume_b(st, s, hh=hh):
                    l8, acc = st
                    p = jnp.exp2(jnp.where(ok_b, s, NEG_BIG))
                    l8 = jnp.concatenate([l8[:, :half], l8[:, half:] + column_sums(p)], axis=1)
                    pv = _dot(vt_ref[0, hh, i, :, half:], p.astype(BF16))
                    return l8, jnp.concatenate([acc[:, :half], acc[:, half:] + pv], axis=1)
                items += [(hh, scores_a, consume_a), (hh, scores_b, consume_b)]
            return pipeline(items, state)

        init = tuple((jnp.zeros((SUBLANES, tq), F32), jnp.zeros((dv, tq), F32)) for _ in range(hs))
        sweep(update, init, diagonal, lambda st: st[1] * (1.0 / jnp.sum(st[0], axis=0, keepdims=True)))

    @pl.when(flag_ref[0] != 1)
    def _():
        def update(st, s, vt):
            m, l8, acc = st
            m_new = jnp.maximum(m, jnp.max(s, axis=0, keepdims=True))
            alpha = jnp.exp2(m - m_new)
            p = jnp.exp2(s - m_new)
            return m_new, alpha * l8 + column_sums(p), alpha * acc + _dot(vt, p.astype(BF16))

        def diagonal(state):
            rows = tq // KEY_SPLIT
            items = []
            for hh in range(hs):
                for r in range(KEY_SPLIT):
                    ok = causal_mask((rows, tq), r * rows)
                    def scores(hh=hh, r=r):
                        kj = k_ref[0, hh, pl.ds(pl.multiple_of(i * tq + r * rows, rows), rows), :]
                        return _dot(kj, q_t[hh])
                    def consume(st, s, hh=hh, r=r, ok=ok):
                        return update(st, jnp.where(ok, s, NEG_BIG), vt_ref[0, hh, i, :, r * rows:(r + 1) * rows])
                    items.append((hh, scores, consume))
            return pipeline(items, state)

        init = tuple((jnp.full((1, tq), NEG_BIG, F32), jnp.zeros((SUBLANES, tq), F32),
                      jnp.zeros((dv, tq), F32)) for _ in range(hs))
        sweep(update, init, diagonal, lambda st: st[2] * (1.0 / jnp.sum(st[1], axis=0, keepdims=True)))


def _attention(flag, q, k, vt, tq, chunk_causal):
    B, H, S, _ = q.shape
    dv = vt.shape[3]
    hs = HEADS_PER_STEP
    assert vt.shape[4] == tq and (hs * dv) % LANES == 0 and H % hs == 0
    nq = S // tq
    return pl.pallas_call(
        functools.partial(_attn_kernel, tq=tq, chunk_causal=chunk_causal),
        grid=(B, H // hs, nq),
        in_specs=[pl.BlockSpec(memory_space=pltpu.SMEM),
                  pl.BlockSpec((1, hs, tq, HEAD_PAD), lambda b, h, i: (b, h, i, 0)),
                  pl.BlockSpec((1, hs, S, HEAD_PAD), lambda b, h, i: (b, h, 0, 0)),
                  pl.BlockSpec((1, hs, nq, dv, tq), lambda b, h, i: (b, h, 0, 0, 0))],
        out_specs=pl.BlockSpec((1, tq, hs * dv), lambda b, h, i: (b, i, h)),
        out_shape=jax.ShapeDtypeStruct((B, S, H * dv), BF16),
        compiler_params=_cparams(3),
        name="attn_mla" if chunk_causal else "attn_fox",
    )(flag, q, k, vt)


def _ssm_kernel(z_ref, xbc_ref, dt_ref, cw_ref, cb_ref, dtb_ref, alog_ref, dx_ref, ng_ref, e_ref,
                o_ref, xpad, hst, *, lc):
    pad = SUBLANES

    @pl.when(pl.program_id(1) == 0)
    def _():
        xpad[0:pad, :] = jnp.zeros((pad, SSM_XBC), F32)
        hst[...] = jnp.zeros_like(hst)

    xpad[pad:pad + lc, :] = xbc_ref[...].astype(F32)
    xp = xpad[...]
    conv = cb_ref[...] + cw_ref[SSM_CONV - 1:SSM_CONV, :] * xp[pad:, :]
    for back in range(1, SSM_CONV):
        kk = SSM_CONV - 1 - back
        conv = conv + cw_ref[kk:kk + 1, :] * pltpu.roll(xp, back, axis=0)[pad:, :]
    xpad[0:pad, :] = xpad[lc:lc + pad, :]
    xc = _silu(conv)
    xs = xc[:, :SSM_INNER]
    bm = xc[:, SSM_INNER:SSM_INNER + SSM_GROUPS * SSM_STATE]
    cm = xc[:, SSM_INNER + SSM_GROUPS * SSM_STATE:]

    lane = lax.broadcasted_iota(jnp.int32, (lc, LANES), 1)
    dt = jnp.where(lane < SSM_HEADS, _softplus(dt_ref[...] + dtb_ref[...]), 0.0)
    a_neg = -jnp.exp(alog_ref[...]) * LOG2E
    a_cum = _dot3_left(_tril_bf16(lc), dt * a_neg)
    head_lanes = lane < SSM_HEADS
    ea = jnp.where(head_lanes, jnp.exp2(a_cum), 0.0)
    dte = jnp.where(head_lanes, jnp.exp2(a_cum[lc - 1:lc, :] - a_cum), 0.0)
    e01 = e_ref[...]

    def expand(v):
        hi, mid, lo = _split3(v)
        packed = (hi.astype(F32) + pltpu.roll(mid.astype(F32), SSM_HEADS, axis=1)
                  + pltpu.roll(lo.astype(F32), 2 * SSM_HEADS, axis=1)).astype(BF16)
        return _dot(packed, e01)

    dt_x = expand(dt)
    ea_x = expand(ea)
    dte_x = expand(dte)
    xdt = xs * dt_x
    a_t = a_cum.T

    row = lax.broadcasted_iota(jnp.int32, (lc, lc), 0)
    col = lax.broadcasted_iota(jnp.int32, (lc, lc), 1)
    causal = row >= col
    lane_first = lane < SSM_HEAD_DIM
    heads_per_group = SSM_HEADS // SSM_GROUPS
    y_blocks = []
    for g in range(SSM_GROUPS):
        bg = bm[:, g * SSM_STATE:(g + 1) * SSM_STATE]
        cg = cm[:, g * SSM_STATE:(g + 1) * SSM_STATE].astype(BF16)
        cb = _dot_nt(cg, bg.astype(BF16))
        gs = slice(g * SSM_GROUP_WIDTH, (g + 1) * SSM_GROUP_WIDTH)
        h_prev = hst[g]
        y_off = _dot(cg, h_prev.astype(BF16)) * ea_x[:, gs]
        xw = (xdt[:, gs] * dte_x[:, gs]).astype(BF16)
        s_new = _dot(bg.T.astype(BF16), xw)
        hst[g] = h_prev * ea_x[lc - 1:lc, gs] + s_new
        for hp in range(heads_per_group // 2):
            t0 = g * SSM_GROUP_WIDTH + hp * LANES
            x_pair = xdt[:, t0:t0 + LANES]
            acc = None
            for e in range(2):
                h = g * heads_per_group + hp * 2 + e
                seg = a_cum[:, h:h + 1] - a_t[h:h + 1, :]
                decay = jnp.exp2(jnp.where(causal, seg, NEG_BIG))
                gmat = (cb * decay).astype(BF16)
                xh = jnp.where(lane_first if e == 0 else ~lane_first, x_pair, 0.0).astype(BF16)
                part = _dot(gmat, xh)
                acc = part if acc is None else acc + part
            y_blocks.append(acc + y_off[:, hp * LANES:(hp + 1) * LANES])
    y = jnp.concatenate(y_blocks, axis=1) + xs * dx_ref[...]
    yg = y * _silu(z_ref[...].astype(F32))
    o_ref[...] = (yg * lax.rsqrt(jnp.mean(yg * yg, axis=-1, keepdims=True) + NORM_EPS)
                  * ng_ref[...]).astype(o_ref.dtype)


def _ssm(u_z, u_xbc, u_dt, cw, cb, dtb, alog, dx, ng, e01, B, S, lc):
    nb = S // lc
    tok = lambda b, i: (b * nb + i, 0)
    const = lambda b, i: (0, 0)
    return pl.pallas_call(
        functools.partial(_ssm_kernel, lc=lc),
        grid=(B, nb),
        in_specs=[pl.BlockSpec((lc, SSM_INNER), tok),
                  pl.BlockSpec((lc, SSM_XBC), tok),
                  pl.BlockSpec((lc, LANES), tok),
                  pl.BlockSpec((SSM_CONV, SSM_XBC), const), pl.BlockSpec((1, SSM_XBC), const),
                  pl.BlockSpec((1, LANES), const), pl.BlockSpec((1, LANES), const),
                  pl.BlockSpec((1, SSM_INNER), const), pl.BlockSpec((1, SSM_INNER), const),
                  _vmem_full()],
        out_specs=pl.BlockSpec((lc, SSM_INNER), tok),
        out_shape=jax.ShapeDtypeStruct((B * S, SSM_INNER), BF16),
        scratch_shapes=[pltpu.VMEM((lc + SUBLANES, SSM_XBC), F32),
                        pltpu.VMEM((SSM_GROUPS, SSM_STATE, SSM_GROUP_WIDTH), F32)],
        compiler_params=_cparams(2),
        name="ssm",
    )(u_z, u_xbc, u_dt, cw, cb, dtb, alog, dx, ng, e01)


def _merge_kernel(x_ref, ya_ref, yb_ref, yc_ref, ug_ref, bg_ref, wa_ref, wb_ref, wc_ref, wo_ref, o_ref):
    ys = (_dot(ya_ref[...], wa_ref[...]), _dot(yb_ref[...], wb_ref[...]), _dot(yc_ref[...], wc_ref[...]))
    merged = None
    for n, y in enumerate(ys):
        cs = slice(n * D_MODEL, (n + 1) * D_MODEL)
        gate = _sigmoid(ug_ref[:, cs].astype(F32) + bg_ref[:, cs])
        merged = gate * y if merged is None else merged + gate * y
    o_ref[...] = x_ref[...] + _dot(merged.astype(BF16), wo_ref[...])


def _merge(x2, ya, yb, yc, ug, bg, wa, wb, wc, wo, tm):
    T = x2.shape[0]
    row = lambda w: pl.BlockSpec((tm, w), lambda i: (i, 0))
    return pl.pallas_call(
        _merge_kernel,
        grid=(T // tm,),
        in_specs=[row(D_MODEL), row(MLA_HEADS * MLA_V), row(FOX_WIDTH), row(SSM_INNER), row(3 * D_MODEL),
                  pl.BlockSpec((1, 3 * D_MODEL), lambda i: (0, 0)),
                  _vmem_full(), _vmem_full(), _vmem_full(), _vmem_full()],
        out_specs=row(D_MODEL),
        out_shape=jax.ShapeDtypeStruct((T, D_MODEL), F32),
        compiler_params=_cparams(1),
        name="merge",
    )(x2, ya, yb, yc, ug, bg, wa, wb, wc, wo)


FF_CH = 256


def _ffn_kernel(x_ref, g_ref, wup_ref, cw_ref, cb_ref, wdn_ref, o_ref, abuf, hbuf, *, tm, blocks_per_seq):
    pad = SUBLANES

    @pl.when(pl.program_id(0) % blocks_per_seq == 0)
    def _():
        abuf[0:pad, :] = jnp.zeros((pad, 2 * D_FF), F32)

    x = x_ref[...]
    hn = (x * lax.rsqrt(jnp.mean(x * x, axis=-1, keepdims=True) + NORM_EPS) * g_ref[...]).astype(BF16)
    for c0 in range(0, D_FF, FF_CH):
        halves = []
        for base in (c0, D_FF + c0):
            cs = slice(base, base + FF_CH)
            abuf[pad:pad + tm, cs] = _dot(hn, wup_ref[:, cs])
            ap = abuf[:, cs]
            conv = cb_ref[:, cs] + cw_ref[FFN_CONV - 1:FFN_CONV, cs] * ap[pad:, :]
            for back in range(1, FFN_CONV):
                kk = FFN_CONV - 1 - back
                conv = conv + cw_ref[kk:kk + 1, cs] * pltpu.roll(ap, back, axis=0)[pad:, :]
            abuf[0:pad, cs] = abuf[tm:tm + pad, cs]
            halves.append(conv)
        hbuf[:, c0:c0 + FF_CH] = (_silu(halves[0]) * halves[1]).astype(BF16)
    o_ref[...] = x + _dot(hbuf[...], wdn_ref[...])


def _ffn(x2, g, wup, cw, cb, wdn, S, tm):
    T = x2.shape[0]
    return pl.pallas_call(
        functools.partial(_ffn_kernel, tm=tm, blocks_per_seq=S // tm),
        grid=(T // tm,),
        in_specs=[pl.BlockSpec((tm, D_MODEL), lambda i: (i, 0)),
                  pl.BlockSpec((1, D_MODEL), lambda i: (0, 0)),
                  _vmem_full(),
                  pl.BlockSpec((FFN_CONV, 2 * D_FF), lambda i: (0, 0)),
                  pl.BlockSpec((1, 2 * D_FF), lambda i: (0, 0)),
                  _vmem_full()],
        out_specs=pl.BlockSpec((tm, D_MODEL), lambda i: (i, 0)),
        out_shape=jax.ShapeDtypeStruct((T, D_MODEL), F32),
        scratch_shapes=[pltpu.VMEM((tm + SUBLANES, 2 * D_FF), F32),
                        pltpu.VMEM((tm, D_FF), BF16)],
        compiler_params=_cparams(1),
        name="ffn",
    )(x2, g, wup, cw, cb, wdn)


def _pad_lanes(v, width=LANES):
    return jnp.pad(v.astype(F32), (0, width - v.shape[0])).reshape(1, width)


def _pack_w_in(w):
    zeros = lambda n: jnp.zeros((D_MODEL, n), BF16)
    seg = lambda a, n: w[:, a:a + n].astype(BF16)
    o_fox = MLA_IN
    o_ssm = MLA_IN + FOX_IN
    o_gate = o_ssm + SSM_IN
    kr0 = MLA_Q_RANK + MLA_KV_RANK
    w_mla = jnp.concatenate([seg(0, kr0), zeros(MLA_NOPE), seg(kr0, MLA_ROPE), zeros(LANES - MLA_QK_DIM)],
                            axis=1)
    w_small = jnp.concatenate([seg(o_fox + 3 * FOX_WIDTH, FOX_HEADS), zeros(LANES - FOX_HEADS),
                               seg(o_ssm + SSM_INNER + SSM_XBC, SSM_HEADS), zeros(LANES - SSM_HEADS)], axis=1)
    assert w_mla.shape[1] == MLA_SEG and w_small.shape[1] == SMALL_W
    return [w_mla, seg(o_fox, 3 * FOX_WIDTH), w_small, seg(o_ssm, SSM_INNER), seg(o_ssm + SSM_INNER, SSM_XBC),
            seg(o_gate, 3 * D_MODEL)]


def _one_hot_row(lanes, value=1.0):
    row = np.zeros((1, LANES), np.float32)
    row[0, lanes] = value
    return row


def _fox_placement():
    rows = np.arange(3 * FOX_HEADS)
    piece = rows // FOX_HEADS
    head = rows % FOX_HEADS
    pq = np.zeros((LANES, FOX_HEADS * HEAD_PAD), np.float32)
    pk = np.zeros((LANES, FOX_HEADS * HEAD_PAD), np.float32)
    pq[rows, head * HEAD_PAD + AUG0 + piece] = 1.0
    pk[rows, head * HEAD_PAD + AUG0 + 3 + piece] = -1.0
    oneq = _one_hot_row(slice(AUG0 + 3, AUG0 + 6))
    onek = _one_hot_row(slice(AUG0, AUG0 + 3))
    return jnp.asarray(pq, BF16), jnp.asarray(pk, BF16), jnp.asarray(oneq), jnp.asarray(onek)


def _shift_rows(bound, lane):
    return -bound * jnp.asarray(_one_hot_row(lane)), jnp.asarray(_one_hot_row(lane))


def _head_expand():
    h = np.arange(3 * SSM_HEADS)
    e = np.zeros((LANES, SSM_INNER), np.float32)
    cols = (h % SSM_HEADS)[:, None] * SSM_HEAD_DIM + np.arange(SSM_HEAD_DIM)[None, :]
    e[h[:, None], cols] = 1.0
    return jnp.asarray(e, BF16)


def kernel(x, positions, norm_mix_g, w_in, b_gate, mla_q_norm_g, mla_w_uq, mla_kv_norm_g, mla_w_ukv,
           mla_q_gain, mla_k_gain, fox_q_gain, fox_k_gain, fox_b_f, ssm_conv_w, ssm_conv_b, ssm_dt_bias,
           ssm_A_log, ssm_D, ssm_norm_g, w_br_mla, w_br_fox, w_br_ssm, w_out, norm_ffn_g, ffn_w_up,
           ffn_conv_w, ffn_conv_b, ffn_w_down):
    B, S, D = x.shape
    T = B * S
    depth = w_in.shape[0]
    tile = min(256, S)
    td = min(DENSE_TILE, S)
    ta = min(ATTN_TILE, S)

    pos = positions.reshape(T, 1).astype(F32)
    cos, sa, sb = _rope_tables(pos, min(ROPE_TILE, S))
    pq, pk, oneq, onek = _fox_placement()
    e01 = _head_expand()
    row = lambda v: v.reshape(1, -1).astype(F32)

    x2 = x.reshape(T, D).astype(F32)
    for l in range(depth):
        wq = jnp.pad(mla_w_uq[l].reshape(MLA_Q_RANK, MLA_HEADS, MLA_QK_DIM),
                     ((0, 0), (0, 0), (0, HEAD_PAD - MLA_QK_DIM))).reshape(MLA_Q_RANK, -1).astype(BF16)
        wkv = mla_w_ukv[l].reshape(MLA_KV_RANK, MLA_HEADS, MLA_NOPE + MLA_V)
        wk = jnp.pad(wkv[:, :, :MLA_NOPE], ((0, 0), (0, 0), (0, HEAD_PAD - MLA_NOPE))
                     ).reshape(MLA_KV_RANK, -1).astype(BF16)
        wv = wkv[:, :, MLA_NOPE:].reshape(MLA_KV_RANK, -1).astype(BF16)
        bound, flag_a = _score_bound(mla_q_gain[l], mla_k_gain[l], MLA_QK_DIM)
        qfix, kfix = _shift_rows(bound, MLA_QK_DIM)
        mla_in = [cos, sa, sb, row(mla_q_norm_g[l]), wq, row(mla_kv_norm_g[l]), wk, wv,
                  _pad_lanes(mla_q_gain[l]), _pad_lanes(mla_k_gain[l]), qfix, kfix]
        bound, flag_b = _score_bound(fox_q_gain[l], fox_k_gain[l], FOX_HEAD_DIM)
        qfix, kfix = _shift_rows(bound, AUG0 + 6)
        fox_in = [_pad_lanes(fox_b_f[l]), _pad_lanes(fox_q_gain[l]), _pad_lanes(fox_k_gain[l]), pq, pk,
                  oneq + qfix, onek + kfix]
        q, k, vt, qf, kf, vtf, u_dt, u_z, u_xbc, u_gate = _front(
            x2, row(norm_mix_g[l]), _pack_w_in(w_in[l]), mla_in, fox_in, B, S, tile, ta)
        y_a = _attention(flag_a, q, k, vt, ta, True).reshape(T, -1)
        y_b = _attention(flag_b, qf, kf, vtf, ta, False).reshape(T, -1)

        dx = jnp.repeat(ssm_D[l].astype(F32), SSM_HEAD_DIM).reshape(1, -1)
        y_c = _ssm(u_z, u_xbc, u_dt, ssm_conv_w[l].astype(F32), row(ssm_conv_b[l]),
                   _pad_lanes(ssm_dt_bias[l]), _pad_lanes(ssm_A_log[l]), dx, row(ssm_norm_g[l]), e01,
                   B, S, tile)

        x2 = _merge(x2, y_a, y_b, y_c, u_gate, row(b_gate[l]), w_br_mla[l].astype(BF16),
                    w_br_fox[l].astype(BF16), w_br_ssm[l].astype(BF16), w_out[l].astype(BF16), td)
        x2 = _ffn(x2, row(norm_ffn_g[l]), ffn_w_up[l].astype(BF16), ffn_conv_w[l].astype(F32),
                  row(ffn_conv_b[l]), ffn_w_down[l].astype(BF16), S, td)
    return x2.reshape(B, S, D).astype(x.dtype)
```

```python
import functools

import numpy as np
import jax
import jax.numpy as jnp
from jax import lax
from jax.experimental import pallas as pl
from jax.experimental.pallas import tpu as pltpu

F32 = jnp.float32
BF16 = jnp.bfloat16

D_MODEL = 1024
NORM_EPS = 1e-6
CHUNK = 64

MLA_HEADS = 8
MLA_Q_RANK = 384
MLA_KV_RANK = 256
MLA_NOPE = 64
MLA_ROPE = 32
MLA_QK_DIM = MLA_NOPE + MLA_ROPE
MLA_V = 64
ROPE_THETA = 10000.0

FOX_HEADS = 8
FOX_HEAD_DIM = 64
FOX_WIDTH = FOX_HEADS * FOX_HEAD_DIM

SSM_HEADS = 16
SSM_HEAD_DIM = 64
SSM_INNER = SSM_HEADS * SSM_HEAD_DIM
SSM_GROUPS = 2
SSM_STATE = 128
SSM_CONV = 4
SSM_XBC = SSM_INNER + 2 * SSM_GROUPS * SSM_STATE
SSM_GROUP_WIDTH = SSM_INNER // SSM_GROUPS

D_FF = 2816
FFN_CONV = 3

MLA_IN = MLA_Q_RANK + MLA_KV_RANK + MLA_ROPE
FOX_IN = 3 * FOX_WIDTH + FOX_HEADS
SSM_IN = SSM_INNER + SSM_XBC + SSM_HEADS

LANES = 128
SUBLANES = 8
HEAD_PAD = LANES
LOG2E = 1.4426950408889634
NEG_BIG = -1e30
VMEM_LIMIT = 52 * 1024 * 1024

MLA_SEG = MLA_Q_RANK + MLA_KV_RANK + LANES
SMALL_W = 2 * LANES

AUG0 = FOX_HEAD_DIM


def _cparams(n_grid):
    return pltpu.CompilerParams(dimension_semantics=("arbitrary",) * n_grid,
                                vmem_limit_bytes=VMEM_LIMIT)


def _vmem_full():
    return pl.BlockSpec(memory_space=pltpu.VMEM)


def _dot(a, b):
    return jnp.dot(a, b, preferred_element_type=F32)


def _dot_nt(a, b):
    return lax.dot_general(a, b, (((1,), (1,)), ((), ())), preferred_element_type=F32)


def _split3(x):
    hi = x.astype(BF16)
    r = x - hi.astype(F32)
    mid = r.astype(BF16)
    lo = (r - mid.astype(F32)).astype(BF16)
    return hi, mid, lo


def _dot3_left(m01, x):
    hi, mid, lo = _split3(x)
    return _dot(m01, hi) + _dot(m01, mid) + _dot(m01, lo)


def _dot3_right(x, m01):
    hi, mid, lo = _split3(x)
    return _dot(hi, m01) + _dot(mid, m01) + _dot(lo, m01)


def _sigmoid(x):
    return 1.0 / (1.0 + jnp.exp(-x))


def _silu(x):
    return x * _sigmoid(x)


def _softplus(x):
    return jnp.maximum(x, 0.0) + jnp.log(1.0 + jnp.exp(-jnp.abs(x)))


def _tril_bf16(n):
    r = lax.broadcasted_iota(jnp.int32, (n, n), 0)
    c = lax.broadcasted_iota(jnp.int32, (n, n), 1)
    return jnp.where(r >= c, 1.0, 0.0).astype(BF16)


ROPE_TILE = 1024

def _rope_kernel(pos_ref, inv_ref, cos_ref, sa_ref, sb_ref):
    ang = pos_ref[...] * inv_ref[...]
    lane = lax.broadcasted_iota(jnp.int32, ang.shape, 1)
    c = jnp.cos(ang)
    s = jnp.sin(ang)
    half = MLA_ROPE // 2
    first = (lane >= MLA_NOPE) & (lane < MLA_NOPE + half)
    second = (lane >= MLA_NOPE + half) & (lane < MLA_QK_DIM)
    cos_ref[...] = jnp.where(lane < MLA_NOPE, 1.0, jnp.where(first | second, c, 0.0))
    sa_ref[...] = jnp.where(first, -s, 0.0)
    sb_ref[...] = jnp.where(second, s, 0.0)


def _rope_tables(pos, ts):
    T = pos.shape[0]
    half = MLA_ROPE // 2
    inv = 1.0 / (ROPE_THETA ** (jnp.arange(0, MLA_ROPE, 2, dtype=F32) / MLA_ROPE))
    inv_lane = jnp.concatenate([jnp.zeros((MLA_NOPE,), F32), inv, inv,
                                jnp.zeros((LANES - MLA_QK_DIM,), F32)]).reshape(1, LANES)
    tab = jax.ShapeDtypeStruct((T, LANES), F32)
    spec = pl.BlockSpec((ts, LANES), lambda i: (i, 0))
    return pl.pallas_call(
        _rope_kernel,
        grid=(T // ts,),
        in_specs=[pl.BlockSpec((ts, 1), lambda i: (i, 0)), pl.BlockSpec((1, LANES), lambda i: (0, 0))],
        out_specs=[spec, spec, spec],
        out_shape=[tab, tab, tab],
        compiler_params=_cparams(1),
        name="rope_tables",
    )(pos, inv_lane)


def _rope(x, cos, sa, sb):
    half = MLA_ROPE // 2
    return x * cos + pltpu.roll(x, LANES - half, axis=1) * sa + pltpu.roll(x, half, axis=1) * sb


IN_CH = 512
DENSE_TILE = 512


def _mla_prep(u, cos_ref, sa_ref, sb_ref, gq_ref, wq_ref, gkv_ref, wk_ref, wv_ref,
              qg_ref, kg_ref, qfix_ref, kfix_ref, q_out, k_out, vt_out):
    cos = cos_ref[...]
    sa = sa_ref[...]
    sb = sb_ref[...]
    cq = u[:, :MLA_Q_RANK]
    ckv = u[:, MLA_Q_RANK:MLA_Q_RANK + MLA_KV_RANK]
    kr = u[:, MLA_Q_RANK + MLA_KV_RANK:]
    cqn = (cq * lax.rsqrt(jnp.mean(cq * cq, axis=-1, keepdims=True) + NORM_EPS) * gq_ref[...]).astype(BF16)
    ckvn = (ckv * lax.rsqrt(jnp.mean(ckv * ckv, axis=-1, keepdims=True) + NORM_EPS) * gkv_ref[...]).astype(BF16)
    q_all = _dot(cqn, wq_ref[...])
    k_all = _dot(ckvn, wk_ref[...])
    v_all = _dot(ckvn, wv_ref[...])
    kr_ss = jnp.sum(kr * kr, axis=-1, keepdims=True)
    kr_rot = _rope(kr * kg_ref[...], cos, sa, sb)
    q_scale = (MLA_QK_DIM ** -0.5) * LOG2E
    for h in range(MLA_HEADS):
        qh = q_all[:, h * HEAD_PAD:(h + 1) * HEAD_PAD]
        ms = jnp.sum(qh * qh, axis=-1, keepdims=True) * (1.0 / MLA_QK_DIM)
        qh = qh * lax.rsqrt(ms + NORM_EPS) * qg_ref[...]
        q_out[0, h] = (_rope(qh, cos, sa, sb) * q_scale + qfix_ref[...]).astype(q_out.dtype)
        kn = k_all[:, h * HEAD_PAD:(h + 1) * HEAD_PAD]
        ms = (jnp.sum(kn * kn, axis=-1, keepdims=True) + kr_ss) * (1.0 / MLA_QK_DIM)
        kh = (kn * kg_ref[...] + kr_rot) * lax.rsqrt(ms + NORM_EPS)
        k_out[0, h] = (kh + kfix_ref[...]).astype(k_out.dtype)
    vt_out[0, :, 0] = _vt_tile(v_all, MLA_HEADS)


def _fox_prep(u, f_raw, bf_ref, qg_ref, kg_ref, pq_ref, pk_ref, oneq_ref, onek_ref,
              q_out, k_out, vt_out, carry):
    ts = u.shape[0]
    lane = lax.broadcasted_iota(jnp.int32, (ts, LANES), 1)
    f = f_raw + bf_ref[...]
    log_f = jnp.minimum(f, 0.0) - jnp.log(1.0 + jnp.exp(-jnp.abs(f)))
    log_f = jnp.where(lane < FOX_HEADS, log_f * LOG2E, 0.0)
    cum = carry[...] + _dot3_left(_tril_bf16(ts), log_f)
    carry[...] = cum[ts - 1:ts, :]
    hi, mid, lo = _split3(cum)
    packed = (hi.astype(F32) + pltpu.roll(mid.astype(F32), FOX_HEADS, axis=1)
              + pltpu.roll(lo.astype(F32), 2 * FOX_HEADS, axis=1)).astype(BF16)
    aug_q = _dot(packed, pq_ref[...])
    aug_k = _dot(packed, pk_ref[...])

    q_scale = (FOX_HEAD_DIM ** -0.5) * LOG2E
    head_lanes = lane < FOX_HEAD_DIM
    for h in range(FOX_HEADS):
        t0 = (h // 2) * LANES
        for src, gain, aug, ones, scale, out in (
                (0, qg_ref, aug_q, oneq_ref, q_scale, q_out),
                (FOX_WIDTH, kg_ref, aug_k, onek_ref, 1.0, k_out)):
            blk = u[:, src + t0:src + t0 + LANES]
            if h % 2 == 1:
                blk = pltpu.roll(blk, FOX_HEAD_DIM, axis=1)
            blk = jnp.where(head_lanes, blk, 0.0)
            ms = jnp.sum(blk * blk, axis=-1, keepdims=True) * (1.0 / FOX_HEAD_DIM)
            blk = blk * lax.rsqrt(ms + NORM_EPS) * (gain[...] * scale)
            blk = blk + aug[:, h * HEAD_PAD:(h + 1) * HEAD_PAD] + ones[...]
            out[0, h] = blk.astype(out.dtype)
    vt_out[0, :, 0] = _vt_tile(u[:, 2 * FOX_WIDTH:], FOX_HEADS)


N_SEG = 6
N_MLA = 12
N_FOX = 7


def _front_kernel(x_ref, g_ref, *refs):
    w_mla, w_fox, w_small, w_z, w_xbc, w_gate = refs[:N_SEG]
    mla_in = refs[N_SEG:N_SEG + N_MLA]
    fox_in = refs[N_SEG + N_MLA:N_SEG + N_MLA + N_FOX]
    outs = refs[N_SEG + N_MLA + N_FOX:]
    q_out, k_out, vt_out, qf_out, kf_out, vtf_out, dt_out, z_out, xbc_out, gate_out, carry = outs

    @pl.when(pl.program_id(1) == 0)
    def _():
        carry[...] = jnp.zeros_like(carry)

    x = x_ref[...]
    hn = (x * lax.rsqrt(jnp.mean(x * x, axis=-1, keepdims=True) + NORM_EPS) * g_ref[...]).astype(BF16)
    u_mla = _dot(hn, w_mla[...])
    u_fox = _dot(hn, w_fox[...])
    u_small = _dot(hn, w_small[...])
    dt_out[...] = u_small[:, LANES:]
    _mla_prep(u_mla, *mla_in, q_out, k_out, vt_out)
    _fox_prep(u_fox, u_small[:, :LANES], *fox_in, qf_out, kf_out, vtf_out, carry)
    for w_ref, o_ref in ((w_z, z_out), (w_xbc, xbc_out), (w_gate, gate_out)):
        n = w_ref.shape[1]
        for cc in range(0, n, IN_CH):
            w = min(IN_CH, n - cc)
            o_ref[:, cc:cc + w] = _dot(hn, w_ref[:, cc:cc + w]).astype(o_ref.dtype)


def _front(x2, g, weights, mla_in, fox_in, B, S, ts, tk):
    nb = S // ts
    T = B * S
    H = MLA_HEADS
    per_tile = tk // ts
    assert len(weights) == N_SEG and len(mla_in) == N_MLA and len(fox_in) == N_FOX and tk % ts == 0
    tok = lambda b, i: (b * nb + i, 0)
    const = lambda b, i: (0, 0)
    row = lambda w: pl.BlockSpec((1, w), const)
    tab = pl.BlockSpec((ts, LANES), tok)
    head_major = pl.BlockSpec((1, H, ts, HEAD_PAD), lambda b, i: (b, 0, i, 0))
    vt_spec = pl.BlockSpec((1, H, 1, V_ROWS, ts), lambda b, i: (b, 0, i // per_tile, 0, i % per_tile))
    qk_shape = jax.ShapeDtypeStruct((B, H, S, HEAD_PAD), BF16)
    vt_shape = jax.ShapeDtypeStruct((B, H, S // tk, V_ROWS, tk), BF16)
    wide = lambda n: (pl.BlockSpec((ts, n), tok), jax.ShapeDtypeStruct((T, n), BF16))
    z_spec, z_shape = wide(SSM_INNER)
    xbc_spec, xbc_shape = wide(SSM_XBC)
    gate_spec, gate_shape = wide(3 * D_MODEL)
    return pl.pallas_call(
        _front_kernel,
        grid=(B, nb),
        in_specs=[pl.BlockSpec((ts, D_MODEL), tok), row(D_MODEL)] + [_vmem_full()] * N_SEG
                 + [tab, tab, tab, row(MLA_Q_RANK), _vmem_full(), row(MLA_KV_RANK), _vmem_full(), _vmem_full(),
                    row(LANES), row(LANES), row(LANES), row(LANES)]
                 + [row(LANES), row(LANES), row(LANES), _vmem_full(), _vmem_full(), row(LANES), row(LANES)],
        out_specs=[head_major, head_major, vt_spec, head_major, head_major, vt_spec,
                   tab, z_spec, xbc_spec, gate_spec],
        out_shape=[qk_shape, qk_shape, vt_shape, qk_shape, qk_shape, vt_shape,
                   jax.ShapeDtypeStruct((T, LANES), F32), z_shape, xbc_shape, gate_shape],
        scratch_shapes=[pltpu.VMEM((1, LANES), F32)],
        compiler_params=_cparams(2),
        name="front",
    )(x2, g, *weights, *mla_in, *fox_in)


ATTN_TILE = 512
V_ROWS = 64


def _vt_tile(v_all, heads):
    ts = v_all.shape[0]
    return v_all.T.reshape(heads, V_ROWS, ts).astype(BF16)


HEADS_PER_STEP = 4
KEY_SPLIT = 1
BOUND_SLACK = 1.02
BOUND_STEP = 0.25
MAX_SAFE_BOUND = 60.0


def _score_bound(q_gain, k_gain, dim):
    bound = dim * jnp.max(jnp.abs(q_gain)) * jnp.max(jnp.abs(k_gain)) * (dim ** -0.5) * LOG2E * BOUND_SLACK
    bound = jnp.ceil(bound.astype(F32) / BOUND_STEP) * BOUND_STEP
    safe = bound < MAX_SAFE_BOUND
    return jnp.where(safe, bound, 0.0), safe.astype(jnp.int32).reshape(1)


def _attn_kernel(flag_ref, q_ref, k_ref, vt_ref, o_ref, *, tq, chunk_causal):
    i = pl.program_id(2)
    hs = q_ref.shape[1]
    dv = vt_ref.shape[3]

    half = tq // 2
    q_t = [q_ref[0, hh].astype(F32).T.astype(BF16) for hh in range(hs)]

    def causal_mask(shape, first_key=0):
        r = lax.broadcasted_iota(jnp.int32, shape, 0) + first_key
        c = lax.broadcasted_iota(jnp.int32, shape, 1)
        shift = CHUNK.bit_length() - 1
        return ((r >> shift) <= (c >> shift)) if chunk_causal else (r <= c)

    def column_sums(p):
        return p.reshape(p.shape[0] // SUBLANES, SUBLANES, p.shape[1]).sum(axis=0)

    def pipeline(items, state):
        state = list(state)
        s_next = items[0][1]()
        for n, (hh, _, consume) in enumerate(items):
            s = s_next
            if n + 1 < len(items):
                s_next = items[n + 1][1]()
            state[hh] = consume(state[hh], s)
        return tuple(state)

    def full_items(js, update):
        items = []
        span = 2 if len(js) % 2 == 0 else 1
        for j in js[::span]:
            for hh in range(hs):
                def scores(j=j, hh=hh):
                    kj = k_ref[0, hh, pl.ds(pl.multiple_of(j * tq, tq), span * tq), :]
                    return _dot(kj, q_t[hh])
                def consume(st, s, j=j, hh=hh):
                    for t in range(span):
                        st = update(st, s[t * tq:(t + 1) * tq], vt_ref[0, hh, j + t])
                    return st
                items.append((hh, scores, consume))
        return items

    def sweep(update, init, diagonal, finish):
        body = lambda n: (lambda jj, st: pipeline(full_items([n * jj + t for t in range(n)], update), st))
        quads = lax.shift_right_logical(i, 2)
        state = lax.fori_loop(0, quads, body(4), init)
        done = quads * 4
        state = lax.fori_loop(0, lax.shift_right_logical(i - done, 1),
                              lambda _, st: pipeline(full_items([done, done + 1], update), st), state)
        state = lax.fori_loop(0, i & 1, lambda _, st: pipeline(full_items([i - 1], update), st), state)
        state = diagonal(state)
        o = jnp.concatenate([finish(st) for st in state], axis=0)
        o_ref[0] = o.T.astype(o_ref.dtype)

    @pl.when(flag_ref[0] == 1)
    def _():
        def update(st, s, vt):
            l8, acc = st
            p = jnp.exp2(s)
            return l8 + column_sums(p), acc + _dot(vt, p.astype(BF16))

        def diagonal(state):
            ok_a = causal_mask((half, tq))
            ok_b = causal_mask((half, half))
            items = []
            for hh in range(hs):
                def scores_a(hh=hh):
                    ka = k_ref[0, hh, pl.ds(pl.multiple_of(i * tq, tq), half), :]
                    return _dot(ka, q_t[hh])
                def consume_a(st, s, hh=hh):
                    return update(st, jnp.where(ok_a, s, NEG_BIG), vt_ref[0, hh, i, :, :half])
                def scores_b(hh=hh):
                    kb = k_ref[0, hh, pl.ds(pl.multiple_of(i * tq + half, half), half), :]
                    return _dot(kb, q_t[hh][:, half:])
                def consume_b(st, s, hh=hh):
                    l8, acc = st
                    p = jnp.exp2(jnp.where(ok_b, s, NEG_BIG))
                    l8 = jnp.concatenate([l8[:, :half], l8[:, half:] + column_sums(p)], axis=1)
                    pv = _dot(vt_ref[0, hh, i, :, half:], p.astype(BF16))
                    return l8, jnp.concatenate([acc[:, :half], acc[:, half:] + pv], axis=1)
                items += [(hh, scores_a, consume_a), (hh, scores_b, consume_b)]
            return pipeline(items, state)

        init = tuple((jnp.zeros((SUBLANES, tq), F32), jnp.zeros((dv, tq), F32)) for _ in range(hs))
        sweep(update, init, diagonal, lambda st: st[1] * (1.0 / jnp.sum(st[0], axis=0, keepdims=True)))

    @pl.when(flag_ref[0] != 1)
    def _():
        def update(st, s, vt):
            m, l8, acc = st
            m_new = jnp.maximum(m, jnp.max(s, axis=0, keepdims=True))
            alpha = jnp.exp2(m - m_new)
            p = jnp.exp2(s - m_new)
            return m_new, alpha * l8 + column_sums(p), alpha * acc + _dot(vt, p.astype(BF16))

        def diagonal(state):
            rows = tq // KEY_SPLIT
            items = []
            for hh in range(hs):
                for r in range(KEY_SPLIT):
                    ok = causal_mask((rows, tq), r * rows)
                    def scores(hh=hh, r=r):
                        kj = k_ref[0, hh, pl.ds(pl.multiple_of(i * tq + r * rows, rows), rows), :]
                        return _dot(kj, q_t[hh])
                    def consume(st, s, hh=hh, r=r, ok=ok):
                        return update(st, jnp.where(ok, s, NEG_BIG), vt_ref[0, hh, i, :, r * rows:(r + 1) * rows])
                    items.append((hh, scores, consume))
            return pipeline(items, state)

        init = tuple((jnp.full((1, tq), NEG_BIG, F32), jnp.zeros((SUBLANES, tq), F32),
                      jnp.zeros((dv, tq), F32)) for _ in range(hs))
        sweep(update, init, diagonal, lambda st: st[2] * (1.0 / jnp.sum(st[1], axis=0, keepdims=True)))


def _attention(flag, q, k, vt, tq, chunk_causal):
    B, H, S, _ = q.shape
    dv = vt.shape[3]
    hs = HEADS_PER_STEP
    assert vt.shape[4] == tq and (hs * dv) % LANES == 0 and H % hs == 0
    nq = S // tq
    return pl.pallas_call(
        functools.partial(_attn_kernel, tq=tq, chunk_causal=chunk_causal),
        grid=(B, H // hs, nq),
        in_specs=[pl.BlockSpec(memory_space=pltpu.SMEM),
                  pl.BlockSpec((1, hs, tq, HEAD_PAD), lambda b, h, i: (b, h, i, 0)),
                  pl.BlockSpec((1, hs, S, HEAD_PAD), lambda b, h, i: (b, h, 0, 0)),
                  pl.BlockSpec((1, hs, nq, dv, tq), lambda b, h, i: (b, h, 0, 0, 0))],
        out_specs=pl.BlockSpec((1, tq, hs * dv), lambda b, h, i: (b, i, h)),
        out_shape=jax.ShapeDtypeStruct((B, S, H * dv), BF16),
        compiler_params=_cparams(3),
        name="attn_mla" if chunk_causal else "attn_fox",
    )(flag, q, k, vt)


def _ssm_kernel(z_ref, xbc_ref, dt_ref, cw_ref, cb_ref, dtb_ref, alog_ref, dx_ref, ng_ref, e_ref,
                o_ref, xpad, hst, *, lc):
    pad = SUBLANES

    @pl.when(pl.program_id(1) == 0)
    def _():
        xpad[0:pad, :] = jnp.zeros((pad, SSM_XBC), F32)
        hst[...] = jnp.zeros_like(hst)

    xpad[pad:pad + lc, :] = xbc_ref[...].astype(F32)
    xp = xpad[...]
    conv = cb_ref[...] + cw_ref[SSM_CONV - 1:SSM_CONV, :] * xp[pad:, :]
    for back in range(1, SSM_CONV):
        kk = SSM_CONV - 1 - back
        conv = conv + cw_ref[kk:kk + 1, :] * pltpu.roll(xp, back, axis=0)[pad:, :]
    xpad[0:pad, :] = xpad[lc:lc + pad, :]
    xc = _silu(conv)
    xs = xc[:, :SSM_INNER]
    bm = xc[:, SSM_INNER:SSM_INNER + SSM_GROUPS * SSM_STATE]
    cm = xc[:, SSM_INNER + SSM_GROUPS * SSM_STATE:]

    lane = lax.broadcasted_iota(jnp.int32, (lc, LANES), 1)
    dt = jnp.where(lane < SSM_HEADS, _softplus(dt_ref[...] + dtb_ref[...]), 0.0)
    a_neg = -jnp.exp(alog_ref[...]) * LOG2E
    a_cum = _dot3_left(_tril_bf16(lc), dt * a_neg)
    head_lanes = lane < SSM_HEADS
    ea = jnp.where(head_lanes, jnp.exp2(a_cum), 0.0)
    dte = jnp.where(head_lanes, jnp.exp2(a_cum[lc - 1:lc, :] - a_cum), 0.0)
    e01 = e_ref[...]

    def expand(v):
        hi, mid, lo = _split3(v)
        packed = (hi.astype(F32) + pltpu.roll(mid.astype(F32), SSM_HEADS, axis=1)
                  + pltpu.roll(lo.astype(F32), 2 * SSM_HEADS, axis=1)).astype(BF16)
        return _dot(packed, e01)

    dt_x = expand(dt)
    ea_x = expand(ea)
    dte_x = expand(dte)
    xdt = xs * dt_x
    a_t = a_cum.T

    row = lax.broadcasted_iota(jnp.int32, (lc, lc), 0)
    col = lax.broadcasted_iota(jnp.int32, (lc, lc), 1)
    causal = row >= col
    lane_first = lane < SSM_HEAD_DIM
    heads_per_group = SSM_HEADS // SSM_GROUPS
    y_blocks = []
    for g in range(SSM_GROUPS):
        bg = bm[:, g * SSM_STATE:(g + 1) * SSM_STATE]
        cg = cm[:, g * SSM_STATE:(g + 1) * SSM_STATE].astype(BF16)
        cb = _dot_nt(cg, bg.astype(BF16))
        gs = slice(g * SSM_GROUP_WIDTH, (g + 1) * SSM_GROUP_WIDTH)
        h_prev = hst[g]
        y_off = _dot(cg, h_prev.astype(BF16)) * ea_x[:, gs]
        xw = (xdt[:, gs] * dte_x[:, gs]).astype(BF16)
        s_new = _dot(bg.T.astype(BF16), xw)
        hst[g] = h_prev * ea_x[lc - 1:lc, gs] + s_new
        for hp in range(heads_per_group // 2):
            t0 = g * SSM_GROUP_WIDTH + hp * LANES
            x_pair = xdt[:, t0:t0 + LANES]
            acc = None
            for e in range(2):
                h = g * heads_per_group + hp * 2 + e
                a_col = a_cum[:, h:h + 1]
                a_row = a_t[h:h + 1, :]
                xh = jnp.where(lane_first if e == 0 else ~lane_first, x_pair, 0.0).astype(BF16)
                hl = lc // 2
                seg_top = a_col[:hl] - a_row[:, :hl]
                g_top = (cb[:hl, :hl] * jnp.exp2(jnp.where(causal[:hl, :hl], seg_top, NEG_BIG))).astype(BF16)
                seg_bot = a_col[hl:] - a_row
                g_bot = (cb[hl:, :] * jnp.exp2(jnp.where(causal[hl:, :], seg_bot, NEG_BIG))).astype(BF16)
                part = jnp.concatenate([_dot(g_top, xh[:hl]), _dot(g_bot, xh)], axis=0)
                acc = part if acc is None else acc + part
            y_blocks.append(acc + y_off[:, hp * LANES:(hp + 1) * LANES])
    y = jnp.concatenate(y_blocks, axis=1) + xs * dx_ref[...]
    yg = y * _silu(z_ref[...].astype(F32))
    o_ref[...] = (yg * lax.rsqrt(jnp.mean(yg * yg, axis=-1, keepdims=True) + NORM_EPS)
                  * ng_ref[...]).astype(o_ref.dtype)


def _ssm(u_z, u_xbc, u_dt, cw, cb, dtb, alog, dx, ng, e01, B, S, lc):
    nb = S // lc
    tok = lambda b, i: (b * nb + i, 0)
    const = lambda b, i: (0, 0)
    return pl.pallas_call(
        functools.partial(_ssm_kernel, lc=lc),
        grid=(B, nb),
        in_specs=[pl.BlockSpec((lc, SSM_INNER), tok),
                  pl.BlockSpec((lc, SSM_XBC), tok),
                  pl.BlockSpec((lc, LANES), tok),
                  pl.BlockSpec((SSM_CONV, SSM_XBC), const), pl.BlockSpec((1, SSM_XBC), const),
                  pl.BlockSpec((1, LANES), const), pl.BlockSpec((1, LANES), const),
                  pl.BlockSpec((1, SSM_INNER), const), pl.BlockSpec((1, SSM_INNER), const),
                  _vmem_full()],
        out_specs=pl.BlockSpec((lc, SSM_INNER), tok),
        out_shape=jax.ShapeDtypeStruct((B * S, SSM_INNER), BF16),
        scratch_shapes=[pltpu.VMEM((lc + SUBLANES, SSM_XBC), F32),
                        pltpu.VMEM((SSM_GROUPS, SSM_STATE, SSM_GROUP_WIDTH), F32)],
        compiler_params=_cparams(2),
        name="ssm",
    )(u_z, u_xbc, u_dt, cw, cb, dtb, alog, dx, ng, e01)


def _merge_kernel(x_ref, ya_ref, yb_ref, yc_ref, ug_ref, bg_ref, wa_ref, wb_ref, wc_ref, wo_ref, o_ref):
    ys = (_dot(ya_ref[...], wa_ref[...]), _dot(yb_ref[...], wb_ref[...]), _dot(yc_ref[...], wc_ref[...]))
    merged = None
    for n, y in enumerate(ys):
        cs = slice(n * D_MODEL, (n + 1) * D_MODEL)
        gate = _sigmoid(ug_ref[:, cs].astype(F32) + bg_ref[:, cs])
        merged = gate * y if merged is None else merged + gate * y
    o_ref[...] = x_ref[...] + _dot(merged.astype(BF16), wo_ref[...])


def _merge(x2, ya, yb, yc, ug, bg, wa, wb, wc, wo, tm):
    T = x2.shape[0]
    row = lambda w: pl.BlockSpec((tm, w), lambda i: (i, 0))
    return pl.pallas_call(
        _merge_kernel,
        grid=(T // tm,),
        in_specs=[row(D_MODEL), row(MLA_HEADS * MLA_V), row(FOX_WIDTH), row(SSM_INNER), row(3 * D_MODEL),
                  pl.BlockSpec((1, 3 * D_MODEL), lambda i: (0, 0)),
                  _vmem_full(), _vmem_full(), _vmem_full(), _vmem_full()],
        out_specs=row(D_MODEL),
        out_shape=jax.ShapeDtypeStruct((T, D_MODEL), F32),
        compiler_params=_cparams(1),
        name="merge",
    )(x2, ya, yb, yc, ug, bg, wa, wb, wc, wo)


FF_CH = 256


def _ffn_kernel(x_ref, g_ref, wup_ref, cw_ref, cb_ref, wdn_ref, o_ref, abuf, hbuf, *, tm, blocks_per_seq):
    pad = SUBLANES

    @pl.when(pl.program_id(0) % blocks_per_seq == 0)
    def _():
        abuf[0:pad, :] = jnp.zeros((pad, 2 * D_FF), F32)

    x = x_ref[...]
    hn = (x * lax.rsqrt(jnp.mean(x * x, axis=-1, keepdims=True) + NORM_EPS) * g_ref[...]).astype(BF16)
    for c0 in range(0, D_FF, FF_CH):
        halves = []
        for base in (c0, D_FF + c0):
            cs = slice(base, base + FF_CH)
            abuf[pad:pad + tm, cs] = _dot(hn, wup_ref[:, cs])
            ap = abuf[:, cs]
            conv = cb_ref[:, cs] + cw_ref[FFN_CONV - 1:FFN_CONV, cs] * ap[pad:, :]
            for back in range(1, FFN_CONV):
                kk = FFN_CONV - 1 - back
                conv = conv + cw_ref[kk:kk + 1, cs] * pltpu.roll(ap, back, axis=0)[pad:, :]
            abuf[0:pad, cs] = abuf[tm:tm + pad, cs]
            halves.append(conv)
        hbuf[:, c0:c0 + FF_CH] = (_silu(halves[0]) * halves[1]).astype(BF16)
    o_ref[...] = x + _dot(hbuf[...], wdn_ref[...])


def _ffn(x2, g, wup, cw, cb, wdn, S, tm):
    T = x2.shape[0]
    return pl.pallas_call(
        functools.partial(_ffn_kernel, tm=tm, blocks_per_seq=S // tm),
        grid=(T // tm,),
        in_specs=[pl.BlockSpec((tm, D_MODEL), lambda i: (i, 0)),
                  pl.BlockSpec((1, D_MODEL), lambda i: (0, 0)),
                  _vmem_full(),
                  pl.BlockSpec((FFN_CONV, 2 * D_FF), lambda i: (0, 0)),
                  pl.BlockSpec((1, 2 * D_FF), lambda i: (0, 0)),
                  _vmem_full()],
        out_specs=pl.BlockSpec((tm, D_MODEL), lambda i: (i, 0)),
        out_shape=jax.ShapeDtypeStruct((T, D_MODEL), F32),
        scratch_shapes=[pltpu.VMEM((tm + SUBLANES, 2 * D_FF), F32),
                        pltpu.VMEM((tm, D_FF), BF16)],
        compiler_params=_cparams(1),
        name="ffn",
    )(x2, g, wup, cw, cb, wdn)


def _pad_lanes(v, width=LANES):
    return jnp.pad(v.astype(F32), (0, width - v.shape[0])).reshape(1, width)


def _pack_w_in(w):
    zeros = lambda n: jnp.zeros((D_MODEL, n), BF16)
    seg = lambda a, n: w[:, a:a + n].astype(BF16)
    o_fox = MLA_IN
    o_ssm = MLA_IN + FOX_IN
    o_gate = o_ssm + SSM_IN
    kr0 = MLA_Q_RANK + MLA_KV_RANK
    w_mla = jnp.concatenate([seg(0, kr0), zeros(MLA_NOPE), seg(kr0, MLA_ROPE), zeros(LANES - MLA_QK_DIM)],
                            axis=1)
    w_small = jnp.concatenate([seg(o_fox + 3 * FOX_WIDTH, FOX_HEADS), zeros(LANES - FOX_HEADS),
                               seg(o_ssm + SSM_INNER + SSM_XBC, SSM_HEADS), zeros(LANES - SSM_HEADS)], axis=1)
    assert w_mla.shape[1] == MLA_SEG and w_small.shape[1] == SMALL_W
    return [w_mla, seg(o_fox, 3 * FOX_WIDTH), w_small, seg(o_ssm, SSM_INNER), seg(o_ssm + SSM_INNER, SSM_XBC),
            seg(o_gate, 3 * D_MODEL)]


def _one_hot_row(lanes, value=1.0):
    row = np.zeros((1, LANES), np.float32)
    row[0, lanes] = value
    return row


def _fox_placement():
    rows = np.arange(3 * FOX_HEADS)
    piece = rows // FOX_HEADS
    head = rows % FOX_HEADS
    pq = np.zeros((LANES, FOX_HEADS * HEAD_PAD), np.float32)
    pk = np.zeros((LANES, FOX_HEADS * HEAD_PAD), np.float32)
    pq[rows, head * HEAD_PAD + AUG0 + piece] = 1.0
    pk[rows, head * HEAD_PAD + AUG0 + 3 + piece] = -1.0
    oneq = _one_hot_row(slice(AUG0 + 3, AUG0 + 6))
    onek = _one_hot_row(slice(AUG0, AUG0 + 3))
    return jnp.asarray(pq, BF16), jnp.asarray(pk, BF16), jnp.asarray(oneq), jnp.asarray(onek)


def _shift_rows(bound, lane):
    return -bound * jnp.asarray(_one_hot_row(lane)), jnp.asarray(_one_hot_row(lane))


def _head_expand():
    h = np.arange(3 * SSM_HEADS)
    e = np.zeros((LANES, SSM_INNER), np.float32)
    cols = (h % SSM_HEADS)[:, None] * SSM_HEAD_DIM + np.arange(SSM_HEAD_DIM)[None, :]
    e[h[:, None], cols] = 1.0
    return jnp.asarray(e, BF16)


def kernel(x, positions, norm_mix_g, w_in, b_gate, mla_q_norm_g, mla_w_uq, mla_kv_norm_g, mla_w_ukv,
           mla_q_gain, mla_k_gain, fox_q_gain, fox_k_gain, fox_b_f, ssm_conv_w, ssm_conv_b, ssm_dt_bias,
           ssm_A_log, ssm_D, ssm_norm_g, w_br_mla, w_br_fox, w_br_ssm, w_out, norm_ffn_g, ffn_w_up,
           ffn_conv_w, ffn_conv_b, ffn_w_down):
    B, S, D = x.shape
    T = B * S
    depth = w_in.shape[0]
    tile = min(256, S)
    td = min(DENSE_TILE, S)
    ta = min(ATTN_TILE, S)

    pos = positions.reshape(T, 1).astype(F32)
    cos, sa, sb = _rope_tables(pos, min(ROPE_TILE, S))
    pq, pk, oneq, onek = _fox_placement()
    e01 = _head_expand()
    row = lambda v: v.reshape(1, -1).astype(F32)

    x2 = x.reshape(T, D).astype(F32)
    for l in range(depth):
        wq = jnp.pad(mla_w_uq[l].reshape(MLA_Q_RANK, MLA_HEADS, MLA_QK_DIM),
                     ((0, 0), (0, 0), (0, HEAD_PAD - MLA_QK_DIM))).reshape(MLA_Q_RANK, -1).astype(BF16)
        wkv = mla_w_ukv[l].reshape(MLA_KV_RANK, MLA_HEADS, MLA_NOPE + MLA_V)
        wk = jnp.pad(wkv[:, :, :MLA_NOPE], ((0, 0), (0, 0), (0, HEAD_PAD - MLA_NOPE))
                     ).reshape(MLA_KV_RANK, -1).astype(BF16)
        wv = wkv[:, :, MLA_NOPE:].reshape(MLA_KV_RANK, -1).astype(BF16)
        bound, flag_a = _score_bound(mla_q_gain[l], mla_k_gain[l], MLA_QK_DIM)
        qfix, kfix = _shift_rows(bound, MLA_QK_DIM)
        mla_in = [cos, sa, sb, row(mla_q_norm_g[l]), wq, row(mla_kv_norm_g[l]), wk, wv,
                  _pad_lanes(mla_q_gain[l]), _pad_lanes(mla_k_gain[l]), qfix, kfix]
        bound, flag_b = _score_bound(fox_q_gain[l], fox_k_gain[l], FOX_HEAD_DIM)
        qfix, kfix = _shift_rows(bound, AUG0 + 6)
        fox_in = [_pad_lanes(fox_b_f[l]), _pad_lanes(fox_q_gain[l]), _pad_lanes(fox_k_gain[l]), pq, pk,
                  oneq + qfix, onek + kfix]
        q, k, vt, qf, kf, vtf, u_dt, u_z, u_xbc, u_gate = _front(
            x2, row(norm_mix_g[l]), _pack_w_in(w_in[l]), mla_in, fox_in, B, S, tile, ta)
        y_a = _attention(flag_a, q, k, vt, ta, True).reshape(T, -1)
        y_b = _attention(flag_b, qf, kf, vtf, ta, False).reshape(T, -1)

        dx = jnp.repeat(ssm_D[l].astype(F32), SSM_HEAD_DIM).reshape(1, -1)
        y_c = _ssm(u_z, u_xbc, u_dt, ssm_conv_w[l].astype(F32), row(ssm_conv_b[l]),
                   _pad_lanes(ssm_dt_bias[l]), _pad_lanes(ssm_A_log[l]), dx, row(ssm_norm_g[l]), e01,
                   B, S, tile)

        x2 = _merge(x2, y_a, y_b, y_c, u_gate, row(b_gate[l]), w_br_mla[l].astype(BF16),
                    w_br_fox[l].astype(BF16), w_br_ssm[l].astype(BF16), w_out[l].astype(BF16), td)
        x2 = _ffn(x2, row(norm_ffn_g[l]), ffn_w_up[l].astype(BF16), ffn_conv_w[l].astype(F32),
                  row(ffn_conv_b[l]), ffn_w_down[l].astype(BF16), S, td)
    return x2.reshape(B, S, D).astype(x.dtype)
```
